```python
import math
import jax, jax.numpy as jnp
from jax import lax
import numpy as np

D_MODEL = 4096
BATCH = 1
SEQ = 16384
DEPTH = 4

FOX_HEAD_DIM = 128
FOX_HEADS = (D_MODEL // 2) // FOX_HEAD_DIM
FOX_WIDTH = FOX_HEADS * FOX_HEAD_DIM
Q_BLOCK = 128
GDN_K_DIM = 128
GDN_V_DIM = 128
GDN_HEADS = (D_MODEL // 2) // GDN_V_DIM
GDN_K_WIDTH = GDN_HEADS * GDN_K_DIM
GDN_V_WIDTH = GDN_HEADS * GDN_V_DIM
GDN_QKV_WIDTH = 2 * GDN_K_WIDTH + GDN_V_WIDTH
CONV_SIZE = 4
CHUNK = 64
MIX_WIDTH = FOX_WIDTH + GDN_V_WIDTH
OFF_FQ = FOX_WIDTH
OFF_FK = OFF_FQ + FOX_WIDTH
OFF_FV = OFF_FK + FOX_WIDTH
OFF_FF = OFF_FV + FOX_HEADS
OFF_GQKV = OFF_FF + GDN_QKV_WIDTH
OFF_GA = OFF_GQKV + GDN_HEADS
OFF_GB = OFF_GA + GDN_HEADS
PROJ_WIDTH = OFF_GB + GDN_V_WIDTH
PROJ_SPLITS = [OFF_FQ, OFF_FK, OFF_FV, OFF_FF, OFF_GQKV, OFF_GA, OFF_GB]
FFN_DENSE = 7168
N_EXPERTS = 8
TOP_K = 2
FFN_EXPERT = 1536
N_DENSE_LAYERS = (DEPTH + 1) // 2
N_MOE_LAYERS = DEPTH // 2
N_MOD = 6
EPS = 1e-6

kernel_name = "hybrid_fox_gdn_moe_sandwich_adaln"


def rms_norm(x, gain):
    xf = x.astype(jnp.float32)
    y = xf * lax.rsqrt(jnp.mean(xf * xf, axis=-1, keepdims=True) + EPS)
    return (y * gain.astype(jnp.float32)).astype(x.dtype)


def l2_normalize(x):
    return x * lax.rsqrt(jnp.sum(x * x, axis=-1, keepdims=True) + EPS)


def causal_conv(x, w):
    k, ch = w.shape
    return lax.conv_general_dilated(
        x, w[:, None, :].astype(x.dtype), window_strides=(1,), padding=[(k - 1, 0)],
        dimension_numbers=('NWC', 'WIO', 'NWC'), feature_group_count=ch)


def forgetting_attention(q, k, v, log_f):
    B, S, H, Dh = q.shape
    nb = S // Q_BLOCK
    scale = Dh ** -0.5
    F = jnp.cumsum(log_f.astype(jnp.float32), axis=1)

    def blocks(t):
        return t.astype(jnp.float32).reshape(B, nb, Q_BLOCK, H, Dh).transpose(1, 0, 3, 2, 4)

    qb = blocks(q) * scale
    kb = blocks(k)
    vb = blocks(v)
    Fb = F.reshape(B, nb, Q_BLOCK, H).transpose(1, 0, 3, 2)
    causal = jnp.tril(jnp.ones((Q_BLOCK, Q_BLOCK), dtype=bool))

    def one_query_block(i):
        qi = qb[i]
        Fq = Fb[i]

        def body(j, carry):
            m, l, acc = carry
            s = jnp.einsum('bhqd,bhkd->bhqk', qi, kb[j]) + Fq[..., :, None] - Fb[j][..., None, :]
            s = jnp.where((j < i) | causal, s, -jnp.inf)
            m_new = jnp.maximum(m, jnp.max(s, axis=-1))
            p = jnp.exp(s - m_new[..., None])
            corr = jnp.exp(m - m_new)
            l = l * corr + jnp.sum(p, axis=-1)
            acc = acc * corr[..., None] + jnp.einsum('bhqk,bhkd->bhqd', p, vb[j])
            return m_new, l, acc

        init = (jnp.full((B, H, Q_BLOCK), -jnp.inf, jnp.float32),
                jnp.zeros((B, H, Q_BLOCK), jnp.float32),
                jnp.zeros((B, H, Q_BLOCK, Dh), jnp.float32))
        _, l, acc = lax.fori_loop(0, i + 1, body, init)
        return acc / l[..., None]

    out = lax.map(one_query_block, jnp.arange(nb))
    return out.transpose(1, 0, 3, 2, 4).reshape(B, S, H, Dh)


def gated_delta_rule(q, k, v, log_alpha, beta):
    B, S, H, Dk = q.shape
    Dv = v.shape[-1]
    N = S // CHUNK
    q = l2_normalize(q) * (Dk ** -0.5)
    k = l2_normalize(k)

    def chunks(t):
        t = t.reshape((B, N, CHUNK, H) + t.shape[3:])
        return jnp.swapaxes(jnp.moveaxis(t, 1, 0), 2, 3)

    q, k, v = chunks(q), chunks(k), chunks(v)
    g = jnp.cumsum(chunks(log_alpha), axis=-1)
    beta = chunks(beta)[..., None]
    idx = jnp.arange(CHUNK)
    incl = idx[:, None] >= idx[None, :]
    strict = idx[:, None] > idx[None, :]
    decay = jnp.exp(jnp.where(incl, g[..., :, None] - g[..., None, :], -jnp.inf))
    k_beta = k * beta
    L = jnp.where(strict, jnp.einsum('nbhcd,nbhsd->nbhcs', k_beta, k) * decay, 0.0)
    T = L + jnp.eye(CHUNK, dtype=L.dtype)
    u = lax.linalg.triangular_solve(T, v * beta, left_side=True, lower=True, unit_diagonal=True)
    w = lax.linalg.triangular_solve(T, k_beta * jnp.exp(g)[..., None], left_side=True,
                                    lower=True, unit_diagonal=True)
    attn = jnp.einsum('nbhcd,nbhsd->nbhcs', q, k) * decay
    q_g = q * jnp.exp(g)[..., None]
    k_tail = k * jnp.exp(g[..., -1:] - g)[..., None]
    a_last = jnp.exp(g[..., -1])[..., None, None]

    def step(state, xs):
        qg_c, w_c, u_c, attn_c, kt_c, al_c = xs
        v_new = u_c - jnp.einsum('bhcd,bhde->bhce', w_c, state)
        o = jnp.einsum('bhcd,bhde->bhce', qg_c, state) + jnp.einsum('bhcs,bhse->bhce', attn_c, v_new)
        state = state * al_c + jnp.einsum('bhcd,bhce->bhde', kt_c, v_new)
        return state, o

    init = jnp.zeros((B, H, Dk, Dv), jnp.float32)
    _, o = lax.scan(step, init, (q_g, w, u, attn, k_tail, a_last))
    return o.transpose(1, 0, 3, 2, 4).reshape(B, S, H, Dv)


def token_mixer(h, w_in, w_out, conv_w, b_f, a_log, dt_bias, fox_norm, gdn_norm):
    B, S, _ = h.shape
    f32 = jnp.float32
    proj = jnp.einsum('bsd,dp->bsp', h, w_in)
    fq, fk, fv, ff, gqkv, ga, gb, gz = jnp.split(proj, PROJ_SPLITS, axis=-1)
    q = fq.reshape(B, S, FOX_HEADS, FOX_HEAD_DIM)
    k = fk.reshape(B, S, FOX_HEADS, FOX_HEAD_DIM)
    v = fv.reshape(B, S, FOX_HEADS, FOX_HEAD_DIM)
    log_f = jax.nn.log_sigmoid(ff.astype(f32) + b_f.astype(f32))
    o_fox = rms_norm(forgetting_attention(q, k, v, log_f), fox_norm).reshape(B, S, FOX_WIDTH)
    qkv = jax.nn.silu(causal_conv(gqkv, conv_w)).astype(f32)
    gq, gk, gv = jnp.split(qkv, [GDN_K_WIDTH, 2 * GDN_K_WIDTH], axis=-1)
    beta = jax.nn.sigmoid(gb.astype(f32))
    log_alpha = -jnp.exp(a_log.astype(f32)) * jax.nn.softplus(ga.astype(f32) + dt_bias.astype(f32))
    o_gdn = gated_delta_rule(gq.reshape(B, S, GDN_HEADS, GDN_K_DIM),
                             gk.reshape(B, S, GDN_HEADS, GDN_K_DIM),
                             gv.reshape(B, S, GDN_HEADS, GDN_V_DIM), log_alpha, beta)
    z = gz.astype(f32).reshape(B, S, GDN_HEADS, GDN_V_DIM)
    o_gdn = (rms_norm(o_gdn, gdn_norm) * jax.nn.silu(z)).reshape(B, S, GDN_V_WIDTH)
    mixed = jnp.concatenate([o_fox.astype(h.dtype), o_gdn.astype(h.dtype)], axis=-1)
    return jnp.einsum('bsm,md->bsd', mixed, w_out)


def swiglu(h, w_gate, w_up, w_down):
    a = jnp.einsum('bsd,df->bsf', h, w_gate)
    b = jnp.einsum('bsd,df->bsf', h, w_up)
    return jnp.einsum('bsf,fd->bsd', jax.nn.silu(a) * b, w_down)


def moe_swiglu(h, w_router, w_gate, w_up, w_down):
    logits = jnp.einsum('bsd,de->bse', h, w_router).astype(jnp.float32)
    top_val, top_idx = lax.top_k(logits, TOP_K)
    top_w = jax.nn.softmax(top_val, axis=-1)
    gates = jnp.sum(jax.nn.one_hot(top_idx, N_EXPERTS, dtype=jnp.float32) * top_w[..., None], axis=-2)
    gates = gates.astype(h.dtype)
    out = jnp.zeros_like(h)
    for e in range(N_EXPERTS):
        out = out + gates[..., e:e + 1] * swiglu(h, w_gate[e], w_up[e], w_down[e])
    return out


def setup_inputs(seed: int = 0) -> dict:
    key = jax.random.key(seed)
    ks = jax.random.split(key, 24)
    f32 = jnp.float32
    nrm = lambda k, shape, s: jax.random.normal(k, shape, f32) * s
    gain = lambda k, shape: 1.0 + 0.02 * jax.random.normal(k, shape, f32)
    x = jax.random.normal(ks[0], (BATCH, SEQ, D_MODEL), f32)
    c = jax.random.normal(ks[1], (BATCH, D_MODEL), f32)
    w_c = nrm(ks[2], (D_MODEL, N_MOD * D_MODEL), 0.5 * D_MODEL ** -0.5)
    b_c = nrm(ks[3], (N_MOD * D_MODEL,), 0.01)
    mod_table = nrm(ks[4], (DEPTH, N_MOD, D_MODEL), 0.1)
    pre_mix_norm = gain(ks[5], (DEPTH, D_MODEL))
    post_mix_norm = gain(ks[6], (DEPTH, D_MODEL))
    pre_ffn_norm = gain(ks[7], (DEPTH, D_MODEL))
    post_ffn_norm = gain(ks[8], (DEPTH, D_MODEL))
    w_in = nrm(ks[9], (DEPTH, D_MODEL, PROJ_WIDTH), D_MODEL ** -0.5)
    w_out = nrm(ks[10], (DEPTH, MIX_WIDTH, D_MODEL), MIX_WIDTH ** -0.5)
    conv_w = nrm(ks[11], (DEPTH, CONV_SIZE, GDN_QKV_WIDTH), CONV_SIZE ** -0.5)
    b_f = jax.random.uniform(ks[12], (DEPTH, FOX_HEADS), f32, 1.0, 5.0)
    a_log = jnp.log(jax.random.uniform(ks[13], (DEPTH, GDN_HEADS), f32, 1.0, 16.0))
    dt = jnp.exp(jax.random.uniform(ks[14], (DEPTH, GDN_HEADS), f32, math.log(1e-3), math.log(0.1)))
    dt_bias = dt + jnp.log(-jnp.expm1(-dt))
    fox_norm = gain(ks[15], (DEPTH, FOX_HEAD_DIM))
    gdn_norm = gain(ks[16], (DEPTH, GDN_V_DIM))
    w_gate_dense = nrm(ks[17], (N_DENSE_LAYERS, D_MODEL, FFN_DENSE), D_MODEL ** -0.5)
    w_up_dense = nrm(ks[18], (N_DENSE_LAYERS, D_MODEL, FFN_DENSE), D_MODEL ** -0.5)
    w_down_dense = nrm(ks[19], (N_DENSE_LAYERS, FFN_DENSE, D_MODEL), FFN_DENSE ** -0.5)
    w_router = nrm(ks[20], (N_MOE_LAYERS, D_MODEL, N_EXPERTS), D_MODEL ** -0.5)
    w_gate_moe = nrm(ks[21], (N_MOE_LAYERS, N_EXPERTS, D_MODEL, FFN_EXPERT), D_MODEL ** -0.5)
    w_up_moe = nrm(ks[22], (N_MOE_LAYERS, N_EXPERTS, D_MODEL, FFN_EXPERT), D_MODEL ** -0.5)
    w_down_moe = nrm(ks[23], (N_MOE_LAYERS, N_EXPERTS, FFN_EXPERT, D_MODEL), FFN_EXPERT ** -0.5)
    return {"x": x, "c": c, "w_c": w_c, "b_c": b_c, "mod_table": mod_table,
            "pre_mix_norm": pre_mix_norm, "post_mix_norm": post_mix_norm,
            "pre_ffn_norm": pre_ffn_norm, "post_ffn_norm": post_ffn_norm,
            "w_in": w_in, "w_out": w_out, "conv_w": conv_w, "b_f": b_f,
            "a_log": a_log, "dt_bias": dt_bias, "fox_norm": fox_norm, "gdn_norm": gdn_norm,
            "w_gate_dense": w_gate_dense, "w_up_dense": w_up_dense, "w_down_dense": w_down_dense,
            "w_router": w_router, "w_gate_moe": w_gate_moe, "w_up_moe": w_up_moe,
            "w_down_moe": w_down_moe}


def reference(x, c, w_c, b_c, mod_table, pre_mix_norm, post_mix_norm, pre_ffn_norm,
              post_ffn_norm, w_in, w_out, conv_w, b_f, a_log, dt_bias, fox_norm, gdn_norm,
              w_gate_dense, w_up_dense, w_down_dense, w_router, w_gate_moe, w_up_moe,
              w_down_moe):
    B = x.shape[0]
    base_mod = (jnp.einsum('bd,dm->bm', jax.nn.silu(c), w_c) + b_c).reshape(B, N_MOD, D_MODEL)
    for l in range(DEPTH):
        mod = (base_mod + mod_table[l])[:, :, None, :]
        shift_m, scale_m, gate_m, shift_f, scale_f, gate_f = (mod[:, i] for i in range(N_MOD))
        h = rms_norm(x, pre_mix_norm[l]) * (1.0 + scale_m) + shift_m
        y = token_mixer(h, w_in[l], w_out[l], conv_w[l], b_f[l], a_log[l], dt_bias[l],
                        fox_norm[l], gdn_norm[l])
        x = x + gate_m * rms_norm(y, post_mix_norm[l])
        h = rms_norm(x, pre_ffn_norm[l]) * (1.0 + scale_f) + shift_f
        if l % 2 == 0:
            j = l // 2
            y = swiglu(h, w_gate_dense[j], w_up_dense[j], w_down_dense[j])
        else:
            j = l // 2
            y = moe_swiglu(h, w_router[j], w_gate_moe[j], w_up_moe[j], w_down_moe[j])
        x = x + gate_f * rms_norm(y, post_ffn_norm[l])
    return x
```

```python
import functools

import jax
import jax.numpy as jnp
from jax import lax
from jax.experimental import pallas as pl
from jax.experimental.pallas import tpu as pltpu

F32 = jnp.float32
BF16 = jnp.bfloat16

HEAD = 128
LANES = 128
N_MOD = 6
EPS = 1e-6
CONV = 4
TOP_K = 2
LOG2E = 1.4426950408889634
NEG = -1e30
GDN_CHUNK = 128
VMEM_LIMIT_BYTES = 56 * 1024 * 1024


def _pick(n, pref, mult=LANES):
    if n <= pref:
        return n
    t = (pref // mult) * mult
    while t >= mult:
        if n % t == 0:
            return t
        t -= mult
    return n


def _cparams(*sem):
    return pltpu.CompilerParams(dimension_semantics=sem, vmem_limit_bytes=VMEM_LIMIT_BYTES)


def _rms(x):
    return x * lax.rsqrt(jnp.mean(x * x, axis=-1, keepdims=True) + EPS)


def _sigmoid(x):
    return 1.0 / (1.0 + jnp.exp(-x))


def _mod_kernel(c_ref, w_ref, b_ref, tab_ref, o_ref):
    c = c_ref[...]
    sc = c * _sigmoid(c)
    lhs = jnp.broadcast_to(sc, (8, sc.shape[1])).astype(BF16)
    base = jnp.dot(lhs, w_ref[...].astype(BF16), preferred_element_type=F32)[0:1]
    o_ref[...] = base + b_ref[...] + tab_ref[...]


def _mod_table(c, w_c, b_c, mod_table):
    depth = mod_table.shape[0]
    d = c.shape[1]
    n = w_c.shape[1]
    tn = _pick(n, 512)
    out = pl.pallas_call(
        _mod_kernel,
        grid=(n // tn,),
        in_specs=[pl.BlockSpec((1, d), lambda j: (0, 0)),
                  pl.BlockSpec((d, tn), lambda j: (0, j)),
                  pl.BlockSpec((1, tn), lambda j: (0, j)),
                  pl.BlockSpec((depth, tn), lambda j: (0, j))],
        out_specs=pl.BlockSpec((depth, tn), lambda j: (0, j)),
        out_shape=jax.ShapeDtypeStruct((depth, n), F32),
        compiler_params=_cparams("arbitrary"),
        name="mod_table",
    )(c, w_c, b_c.reshape(1, n), mod_table.reshape(depth, n))
    return out.reshape(depth, N_MOD, d)


def _prenorm_kernel(x_ref, g_ref, mod_ref, h_ref, *, shift_row, scale_row):
    h = _rms(x_ref[...]) * g_ref[...]
    h = h * (1.0 + mod_ref[scale_row:scale_row + 1, :]) + mod_ref[shift_row:shift_row + 1, :]
    h_ref[...] = h.astype(h_ref.dtype)


def _prenorm(x, gain, mod, shift_row, scale_row, out_dtype):
    s, d = x.shape
    tm = _pick(s, 256, 8)
    return pl.pallas_call(
        functools.partial(_prenorm_kernel, shift_row=shift_row, scale_row=scale_row),
        grid=(s // tm,),
        in_specs=[pl.BlockSpec((tm, d), lambda i: (i, 0)),
                  pl.BlockSpec((1, d), lambda i: (0, 0)),
                  pl.BlockSpec((N_MOD, d), lambda i: (0, 0))],
        out_specs=pl.BlockSpec((tm, d), lambda i: (i, 0)),
        out_shape=jax.ShapeDtypeStruct((s, d), out_dtype),
        compiler_params=_cparams("parallel"),
        name="prenorm",
    )(x, gain.reshape(1, d), mod)


def _resid_update(x, y, gpost, mod, gate_row):
    yn = _rms(y) * gpost
    return x + mod[gate_row:gate_row + 1, :] * yn


def _next_h(x, gpre, mod_next, shift_row, scale_row):
    h = _rms(x) * gpre
    return h * (1.0 + mod_next[scale_row:scale_row + 1, :]) + mod_next[shift_row:shift_row + 1, :]


def _resid_kernel(x_ref, y_ref, gpost_ref, mod_ref, gpre_ref, modn_ref, xo_ref, *h_refs,
                  gate_row, shift_row, scale_row):
    x = _resid_update(x_ref[...], y_ref[...].astype(F32), gpost_ref[...], mod_ref[...], gate_row)
    xo_ref[...] = x
    if h_refs:
        h_ref, = h_refs
        h_ref[...] = _next_h(x, gpre_ref[...], modn_ref[...], shift_row, scale_row).astype(h_ref.dtype)


def _resid(x, y, gpost, mod, gate_row, gpre=None, mod_next=None, shift_row=0, scale_row=0, h_dtype=None):
    s, d = x.shape
    tm = _pick(s, 256, 8)
    want_h = h_dtype is not None
    if not want_h:
        gpre, mod_next = gpost, mod
    row = pl.BlockSpec((tm, d), lambda i: (i, 0))
    vec = pl.BlockSpec((1, d), lambda i: (0, 0))
    tab = pl.BlockSpec((N_MOD, d), lambda i: (0, 0))
    out_shape = [jax.ShapeDtypeStruct((s, d), F32)]
    out_specs = [row]
    if want_h:
        out_shape.append(jax.ShapeDtypeStruct((s, d), h_dtype))
        out_specs.append(row)
    res = pl.pallas_call(
        functools.partial(_resid_kernel, gate_row=gate_row, shift_row=shift_row, scale_row=scale_row),
        grid=(s // tm,),
        in_specs=[row, row, vec, tab, vec, tab],
        out_specs=out_specs,
        out_shape=out_shape,
        compiler_params=_cparams("parallel"),
        name="resid",
    )(x, y, gpost.reshape(1, d), mod, gpre.reshape(1, d), mod_next)
    return (res[0], res[1]) if want_h else (res[0], None)


def _mm_kernel(a_ref, b_ref, o_ref):
    o_ref[...] = jnp.dot(a_ref[...], b_ref[...], preferred_element_type=F32).astype(o_ref.dtype)


def _matmul(a, b, out_dtype, tm_pref=1024, tn_pref=1024, name="matmul"):
    m, k = a.shape
    n = b.shape[1]
    tm = _pick(m, tm_pref)
    tn = _pick(n, tn_pref)
    return pl.pallas_call(
        _mm_kernel,
        grid=(m // tm, n // tn),
        in_specs=[pl.BlockSpec((tm, k), lambda i, j: (i, 0)),
                  pl.BlockSpec((k, tn), lambda i, j: (0, j))],
        out_specs=pl.BlockSpec((tm, tn), lambda i, j: (i, j)),
        out_shape=jax.ShapeDtypeStruct((m, n), out_dtype),
        compiler_params=_cparams("parallel", "arbitrary"),
        name=name,
    )(a, b)


def _mm2_kernel(a1_ref, a2_ref, b1_ref, b2_ref, o_ref):
    acc = jnp.dot(a1_ref[...], b1_ref[...], preferred_element_type=F32)
    acc = acc + jnp.dot(a2_ref[...], b2_ref[...], preferred_element_type=F32)
    o_ref[...] = acc.astype(o_ref.dtype)


def _out_proj(a1, a2, w, out_dtype):
    m, k1 = a1.shape
    k2 = a2.shape[1]
    assert k1 == k2 and w.shape[0] == k1 + k2
    n = w.shape[1]
    tm = _pick(m, 1024)
    tn = _pick(n, 1024)
    return pl.pallas_call(
        _mm2_kernel,
        grid=(m // tm, n // tn),
        in_specs=[pl.BlockSpec((tm, k1), lambda i, j: (i, 0)),
                  pl.BlockSpec((tm, k2), lambda i, j: (i, 0)),
                  pl.BlockSpec((k1, tn), lambda i, j: (0, j)),
                  pl.BlockSpec((k2, tn), lambda i, j: (1, j))],
        out_specs=pl.BlockSpec((tm, tn), lambda i, j: (i, j)),
        out_shape=jax.ShapeDtypeStruct((m, n), out_dtype),
        compiler_params=_cparams("parallel", "arbitrary"),
        name="out_proj",
    )(a1, a2, w, w)


def _scan_kernel(s_ref, bias_ref, alog_ref, o_ref, ot_ref, carry_ref, *, heads, rows):
    @pl.when(pl.program_id(0) == 0)
    def _():
        carry_ref[...] = jnp.zeros_like(carry_ref)

    h = heads
    lane = lax.broadcasted_iota(jnp.int32, (1, LANES), 1)
    ri = lax.broadcasted_iota(jnp.int32, (GDN_CHUNK, GDN_CHUNK), 0)
    ci = lax.broadcasted_iota(jnp.int32, (GDN_CHUNK, GDN_CHUNK), 1)
    tri = jnp.where(ri >= ci, 1.0, 0.0).astype(F32)
    a_exp = jnp.exp(alog_ref[...])
    for sb in range(rows // GDN_CHUNK):
        r0 = sb * GDN_CHUNK
        x = s_ref[r0:r0 + GDN_CHUNK, :] + bias_ref[...]
        soft = jnp.log(1.0 + jnp.exp(-jnp.abs(x)))
        log_sig = jnp.minimum(x, 0.0) - soft
        softplus = jnp.maximum(x, 0.0) + soft
        val = jnp.where(lane < h, log_sig * LOG2E, jnp.where(lane < 2 * h, -a_exp * softplus, 0.0))
        cs = jnp.dot(tri, val, preferred_element_type=F32, precision=lax.Precision.HIGHEST)
        run = cs + carry_ref[...]
        carry_ref[...] = run[GDN_CHUNK - 1:GDN_CHUNK, :]
        g_last = jnp.broadcast_to(cs[GDN_CHUNK - 1:GDN_CHUNK, :], cs.shape)
        g_last = pltpu.roll(g_last, 2 * h, axis=1)
        out = jnp.where(lane < h, run,
                        jnp.where(lane < 2 * h, cs,
                                  jnp.where(lane < 3 * h, _sigmoid(x),
                                            jnp.where(lane < 4 * h, g_last, 0.0))))
        o_ref[r0:r0 + GDN_CHUNK, :] = out
        ot_ref[:, r0:r0 + GDN_CHUNK] = out.T


def _gate_scalars(small, b_f, a_log, dt_bias):
    s = small.shape[0]
    h = b_f.shape[0]
    assert 4 * h <= LANES and s % GDN_CHUNK == 0
    rows = _pick(s, 512, GDN_CHUNK)
    pad = jnp.zeros((LANES - 2 * h,), F32)
    bias = jnp.concatenate([b_f, dt_bias, pad]).reshape(1, LANES)
    alog = jnp.concatenate([jnp.zeros((h,), F32), a_log, pad]).reshape(1, LANES)
    return pl.pallas_call(
        functools.partial(_scan_kernel, heads=h, rows=rows),
        grid=(s // rows,),
        in_specs=[pl.BlockSpec((rows, LANES), lambda i: (i, 0)),
                  pl.BlockSpec((1, LANES), lambda i: (0, 0)),
                  pl.BlockSpec((1, LANES), lambda i: (0, 0))],
        out_specs=[pl.BlockSpec((rows, LANES), lambda i: (i, 0)),
                   pl.BlockSpec((LANES, rows), lambda i: (0, i))],
        out_shape=[jax.ShapeDtypeStruct((s, LANES), F32),
                   jax.ShapeDtypeStruct((LANES, s), F32)],
        scratch_shapes=[pltpu.VMEM((1, LANES), F32)],
        compiler_params=_cparams("arbitrary"),
        name="gate_scalars",
    )(small, bias, alog)


def _fox_prep_kernel(q_ref, k_ref, v_ref, p_ref, qa_ref, ka_ref, vt_ref, *, heads):
    tm = q_ref.shape[0]
    scale = HEAD ** -0.5 * LOG2E
    lane = lax.broadcasted_iota(jnp.int32, (tm, HEAD), 1)
    row = lax.broadcasted_iota(jnp.int32, (HEAD, tm), 0)
    ones_rows = jnp.where(row < 3, 1.0, 0.0).astype(BF16)
    p = p_ref[...]
    for h in range(heads):
        sl = slice(h * HEAD, (h + 1) * HEAD)
        qa_ref[h, 0:HEAD, :] = (q_ref[:, sl].astype(F32) * scale).T.astype(BF16)
        qa_ref[h, HEAD:2 * HEAD, :] = ones_rows
        vt_ref[h, 0] = v_ref[:, sl].astype(F32).T.astype(BF16)
        f_col = p[:, h:h + 1]
        e = f_col[0:1, :] - f_col
        e_hi = e.astype(BF16).astype(F32)
        e_mid = (e - e_hi).astype(BF16).astype(F32)
        e_lo = e - e_hi - e_mid
        cols = jnp.where(lane == 0, e_hi, jnp.where(lane == 1, e_mid, jnp.where(lane == 2, e_lo, 0.0)))
        ka_ref[h, 0, :, 0:HEAD] = k_ref[:, sl]
        ka_ref[h, 0, :, HEAD:2 * HEAD] = cols.astype(BF16)


def _fox_prep(proj, p, heads, blk):
    s = proj.shape[0]
    w = heads * HEAD
    return pl.pallas_call(
        functools.partial(_fox_prep_kernel, heads=heads),
        grid=(s // blk,),
        in_specs=[pl.BlockSpec((blk, w), lambda i: (i, 0)),
                  pl.BlockSpec((blk, w), lambda i: (i, 1)),
                  pl.BlockSpec((blk, w), lambda i: (i, 2)),
                  pl.BlockSpec((blk, LANES), lambda i: (i, 0))],
        out_specs=[pl.BlockSpec((heads, 2 * HEAD, blk), lambda i: (0, 0, i)),
                   pl.BlockSpec((heads, 1, blk, 2 * HEAD), lambda i: (0, i, 0, 0)),
                   pl.BlockSpec((heads, 1, HEAD, blk), lambda i: (0, i, 0, 0))],
        out_shape=[jax.ShapeDtypeStruct((heads, 2 * HEAD, s), BF16),
                   jax.ShapeDtypeStruct((heads, s // blk, blk, 2 * HEAD), BF16),
                   jax.ShapeDtypeStruct((heads, s // blk, HEAD, blk), BF16)],
        compiler_params=_cparams("parallel"),
        name="fox_prep",
    )(proj, proj, proj, p)


def _fox_kernel(fs_ref, qa_ref, ka_ref, vt_ref, gain_ref, o_ref, m_ref, l_ref, acc_ref, *, blk):
    h = pl.program_id(0)
    i = pl.program_id(1)
    qa = qa_ref[0]
    m_ref[...] = jnp.full_like(m_ref, NEG)
    l_ref[...] = jnp.zeros_like(l_ref)
    acc_ref[...] = jnp.zeros_like(acc_ref)
    f_q = fs_ref[h, i]

    def step(j, diagonal):
        ka = ka_ref[0, j]
        s = jnp.dot(ka, qa, preferred_element_type=F32)
        if diagonal:
            kr = lax.broadcasted_iota(jnp.int32, s.shape, 0)
            qc = lax.broadcasted_iota(jnp.int32, s.shape, 1)
            s = jnp.where(kr <= qc, s, NEG)
        c = f_q - fs_ref[h, j]
        m_old = m_ref[...]
        m_new = jnp.maximum(m_old, jnp.max(s, axis=0, keepdims=True) + c)
        p = jnp.exp2(s - (m_new - c))
        alpha = jnp.exp2(m_old - m_new)
        l_ref[...] = alpha * l_ref[...] + jnp.sum(p, axis=0, keepdims=True)
        vt = vt_ref[0, j]
        acc_ref[...] = alpha * acc_ref[...] + jnp.dot(vt, p.astype(BF16), preferred_element_type=F32)
        m_ref[...] = m_new

    def body(j, carry):
        step(j, False)
        return carry

    lax.fori_loop(0, i, body, 0)
    step(i, True)
    out = acc_ref[...] / l_ref[...]
    out = out * lax.rsqrt(jnp.mean(out * out, axis=0, keepdims=True) + EPS)
    o_ref[...] = (out.T * gain_ref[...]).astype(o_ref.dtype)


def _fox_attention(qa, ka, vt, fs, gain, blk):
    heads, _, s = qa.shape
    nq = s // blk
    grid_spec = pltpu.PrefetchScalarGridSpec(
        num_scalar_prefetch=1,
        grid=(heads, nq),
        in_specs=[pl.BlockSpec((1, 2 * HEAD, blk), lambda h, i, fs: (h, 0, i)),
                  pl.BlockSpec((1, nq, blk, 2 * HEAD), lambda h, i, fs: (h, 0, 0, 0)),
                  pl.BlockSpec((1, nq, HEAD, blk), lambda h, i, fs: (h, 0, 0, 0)),
                  pl.BlockSpec((1, HEAD), lambda h, i, fs: (0, 0))],
        out_specs=pl.BlockSpec((blk, HEAD), lambda h, i, fs: (i, h)),
        scratch_shapes=[pltpu.VMEM((1, blk), F32), pltpu.VMEM((1, blk), F32), pltpu.VMEM((HEAD, blk), F32)],
    )
    return pl.pallas_call(
        functools.partial(_fox_kernel, blk=blk),
        grid_spec=grid_spec,
        out_shape=jax.ShapeDtypeStruct((s, heads * HEAD), BF16),
        compiler_params=_cparams("parallel", "arbitrary"),
        name="fox_attention",
    )(fs, qa, ka, vt, gain.reshape(1, HEAD))


def _gdn_prep_kernel(x_ref, prev_ref, w_ref, o_ref, buf_ref, *, heads):
    tm = x_ref.shape[0]
    halo = prev_ref.shape[0]
    prev = prev_ref[...].astype(F32)
    buf_ref[0:halo, :] = jnp.where(pl.program_id(0) == 0, 0.0, prev)
    buf_ref[halo:halo + tm, :] = x_ref[...].astype(F32)
    y = jnp.zeros(x_ref.shape, F32)
    for s in range(CONV):
        y = y + buf_ref[halo - s:halo - s + tm, :] * w_ref[CONV - 1 - s:CONV - s, :]
    y = y * _sigmoid(y)
    w = heads * HEAD
    for h in range(2 * heads):
        sl = slice(h * HEAD, (h + 1) * HEAD)
        v = y[:, sl]
        n = v * lax.rsqrt(jnp.sum(v * v, axis=-1, keepdims=True) + EPS)
        if h < heads:
            n = n * HEAD ** -0.5
        o_ref[:, sl] = n.astype(o_ref.dtype)
    o_ref[:, 2 * w:3 * w] = y[:, 2 * w:3 * w].astype(o_ref.dtype)


def _gdn_prep(proj, conv_w, heads):
    s = proj.shape[0]
    w3 = 3 * heads * HEAD
    tm = _pick(s, 256, 16)
    halo = 16
    per = tm // halo
    return pl.pallas_call(
        functools.partial(_gdn_prep_kernel, heads=heads),
        grid=(s // tm,),
        in_specs=[pl.BlockSpec((tm, w3), lambda i: (i, 1)),
                  pl.BlockSpec((halo, w3), lambda i: (jnp.maximum(i * per - 1, 0), 1)),
                  pl.BlockSpec((CONV, w3), lambda i: (0, 0))],
        out_specs=pl.BlockSpec((tm, w3), lambda i: (i, 0)),
        out_shape=jax.ShapeDtypeStruct((s, w3), BF16),
        scratch_shapes=[pltpu.VMEM((tm + halo, w3), F32)],
        compiler_params=_cparams("parallel"),
        name="gdn_prep",
    )(proj, proj, conv_w)


def _bdot(a, b):
    return jnp.dot(a.astype(BF16), b.astype(BF16), preferred_element_type=F32)


def _bdot_nt(a, b):
    return lax.dot_general(a.astype(BF16), b.astype(BF16), (((1,), (1,)), ((), ())),
                           preferred_element_type=F32)


def _gdn_kernel(q_ref, k_ref, v_ref, z_ref, pt_ref, gain_ref, o_ref, state_ref, *, heads, group):
    hg = pl.program_id(0)

    @pl.when(pl.program_id(1) == 0)
    def _():
        state_ref[...] = jnp.zeros_like(state_ref)

    tb = q_ref.shape[0]
    c = GDN_CHUNK
    ri = lax.broadcasted_iota(jnp.int32, (tb, tb), 0)
    ci = lax.broadcasted_iota(jnp.int32, (tb, tb), 1)
    rc_xor = jnp.bitwise_xor(ri, ci)
    same_chunk = (rc_xor >> (c.bit_length() - 1)) == 0
    causal = same_chunk & (ri >= ci)
    eye = jnp.where(ri == ci, 1.0, 0.0).astype(F32)

    def col(r):
        return jnp.broadcast_to(r, (HEAD, tb)).T

    for gi in range(group):
        head = hg * group + gi
        sl = slice(gi * HEAD, (gi + 1) * HEAD)
        g_row = pt_ref[pl.ds(heads + head, 1), :]
        b_row = pt_ref[pl.ds(2 * heads + head, 1), :]
        gl_row = pt_ref[pl.ds(3 * heads + head, 1), :]
        g_col = col(g_row)
        b_col = col(b_row)
        e_g = jnp.exp(g_col)
        e_tail = jnp.exp(col(gl_row) - g_col)
        q = q_ref[:, sl].astype(F32)
        k = k_ref[:, sl].astype(F32)
        v = v_ref[:, sl].astype(F32)
        kb = k * b_col
        decay = jnp.exp(jnp.where(causal, g_col[:, 0:1] - g_row, NEG))
        lower = jnp.where(ri > ci, _bdot_nt(kb, k) * decay, 0.0)
        attn = _bdot_nt(q, k) * decay
        inv = eye - jnp.where(rc_xor == 1, lower, 0.0)
        b = 2
        while b < c:
            off = jnp.where((rc_xor >> (b.bit_length() - 1)) == 1, lower, 0.0)
            inv = inv - _bdot(inv, _bdot(off, inv))
            b *= 2
        uw = _bdot(inv, jnp.concatenate([v * b_col, kb * e_g], axis=1))
        u = uw[:, 0:HEAD]
        w = uw[:, HEAD:2 * HEAD]
        qg = q * e_g
        kt_t = (k * e_tail).T
        state = state_ref[gi]
        outs = []
        for ch in range(tb // c):
            r = slice(ch * c, (ch + 1) * c)
            x = _bdot(jnp.concatenate([w[r], qg[r]], axis=0), state)
            v_new = u[r] - x[0:c]
            outs.append(x[c:2 * c] + _bdot(attn[r, r], v_new))
            a_last = jnp.exp(gl_row[:, ch * c:ch * c + 1])
            state = state * a_last + _bdot(kt_t[:, r], v_new)
        state_ref[gi] = state
        o = jnp.concatenate(outs, axis=0)
        z = z_ref[:, sl].astype(F32)
        o = _rms(o) * gain_ref[...] * (z * _sigmoid(z))
        o_ref[:, sl] = o.astype(o_ref.dtype)


def _gdn(qkv, proj, pt, gain, heads, group=2):
    s = qkv.shape[0]
    tb = _pick(s, 256, GDN_CHUNK)
    group = min(group, heads)
    gw = group * HEAD
    per = heads // group
    return pl.pallas_call(
        functools.partial(_gdn_kernel, heads=heads, group=group),
        grid=(per, s // tb),
        in_specs=[pl.BlockSpec((tb, gw), lambda g, i: (i, g)),
                  pl.BlockSpec((tb, gw), lambda g, i: (i, per + g)),
                  pl.BlockSpec((tb, gw), lambda g, i: (i, 2 * per + g)),
                  pl.BlockSpec((tb, gw), lambda g, i: (i, 6 * per + g)),
                  pl.BlockSpec((LANES, tb), lambda g, i: (0, i)),
                  pl.BlockSpec((1, HEAD), lambda g, i: (0, 0))],
        out_specs=pl.BlockSpec((tb, gw), lambda g, i: (i, g)),
        out_shape=jax.ShapeDtypeStruct((s, heads * HEAD), BF16),
        scratch_shapes=[pltpu.VMEM((group, HEAD, HEAD), F32)],
        compiler_params=_cparams("parallel", "arbitrary"),
        name="gdn",
    )(qkv, qkv, qkv, proj, pt, gain.reshape(1, HEAD))


def _ffn_kernel(h_ref, wg_ref, wu_ref, wd_ref, o_ref, acc_ref):
    f = pl.program_id(1)
    h = h_ref[...]
    a = jnp.dot(h, wg_ref[...], preferred_element_type=F32)
    b = jnp.dot(h, wu_ref[...], preferred_element_type=F32)
    part = jnp.dot((a * _sigmoid(a) * b).astype(BF16), wd_ref[...], preferred_element_type=F32)

    @pl.when(f == 0)
    def _():
        acc_ref[...] = part

    @pl.when(f > 0)
    def _():
        acc_ref[...] += part

    @pl.when(f == pl.num_programs(1) - 1)
    def _():
        o_ref[...] = acc_ref[...].astype(o_ref.dtype)


def _ffn_dense(h, wg, wu, wd, out_dtype):
    s, d = h.shape
    f = wg.shape[1]
    tm = _pick(s, 512)
    tf = _pick(f, 256)
    return pl.pallas_call(
        _ffn_kernel,
        grid=(s // tm, f // tf),
        in_specs=[pl.BlockSpec((tm, d), lambda i, j: (i, 0)),
                  pl.BlockSpec((d, tf), lambda i, j: (0, j)),
                  pl.BlockSpec((d, tf), lambda i, j: (0, j)),
                  pl.BlockSpec((tf, d), lambda i, j: (j, 0))],
        out_specs=pl.BlockSpec((tm, d), lambda i, j: (i, 0)),
        out_shape=jax.ShapeDtypeStruct((s, d), out_dtype),
        scratch_shapes=[pltpu.VMEM((tm, d), F32)],
        compiler_params=_cparams("parallel", "arbitrary"),
        name="ffn_dense",
    )(h, wg, wu, wd)


def _router_kernel(h_ref, w_ref, o_ref, *, experts):
    logits = jnp.dot(h_ref[...].astype(BF16), w_ref[...], preferred_element_type=F32)
    lane = lax.broadcasted_iota(jnp.int32, logits.shape, 1).astype(F32)
    logits = jnp.where(lane < experts, logits, -jnp.inf)
    m1 = jnp.max(logits, axis=-1, keepdims=True)
    i1 = jnp.min(jnp.where(logits == m1, lane, LANES), axis=-1, keepdims=True)
    rest = jnp.where(lane == i1, -jnp.inf, logits)
    m2 = jnp.max(rest, axis=-1, keepdims=True)
    i2 = jnp.min(jnp.where(rest == m2, lane, LANES), axis=-1, keepdims=True)
    e2 = jnp.exp(m2 - m1)
    w1 = 1.0 / (1.0 + e2)
    w2 = e2 / (1.0 + e2)
    hit = jnp.where((lane == i1) | (lane == i2), 1.0, 0.0)
    info = jnp.where(lane == experts, i1,
                     jnp.where(lane == experts + 1, i2,
                               jnp.where(lane == experts + 2, w1,
                                         jnp.where(lane == experts + 3, w2, hit))))
    o_ref[...] = info


def _router(h, w_router):
    s, d = h.shape
    e = w_router.shape[1]
    assert e + 4 <= LANES
    wr = jnp.zeros((d, LANES), BF16).at[:, :e].set(w_router.astype(BF16))
    tm = _pick(s, 256, 8)
    return pl.pallas_call(
        functools.partial(_router_kernel, experts=e),
        grid=(s // tm,),
        in_specs=[pl.BlockSpec((tm, d), lambda i: (i, 0)),
                  pl.BlockSpec((d, LANES), lambda i: (0, 0))],
        out_specs=pl.BlockSpec((tm, LANES), lambda i: (i, 0)),
        out_shape=jax.ShapeDtypeStruct((s, LANES), F32),
        compiler_params=_cparams("parallel"),
        name="router",
    )(h, wr)


def _count_kernel(r_ref, o_ref, carry_ref, *, experts):
    @pl.when(pl.program_id(0) == 0)
    def _():
        carry_ref[...] = jnp.zeros_like(carry_ref)

    rows = r_ref.shape[0]
    lane = lax.broadcasted_iota(jnp.int32, (1, LANES), 1)
    ri = lax.broadcasted_iota(jnp.int32, (rows, rows), 0)
    ci = lax.broadcasted_iota(jnp.int32, (rows, rows), 1)
    tri = jnp.where(ri >= ci, 1.0, 0.0).astype(BF16)
    hit = jnp.where(lane < experts, r_ref[...], 0.0)
    run = jnp.dot(tri, hit.astype(BF16), preferred_element_type=F32) + carry_ref[...]
    carry_ref[...] = run[rows - 1:rows, :]
    o_ref[...] = run


def _running_counts(info, experts):
    s = info.shape[0]
    rows = _pick(s, 256, 8)
    return pl.pallas_call(
        functools.partial(_count_kernel, experts=experts),
        grid=(s // rows,),
        in_specs=[pl.BlockSpec((rows, LANES), lambda i: (i, 0))],
        out_specs=pl.BlockSpec((rows, LANES), lambda i: (i, 0)),
        out_shape=jax.ShapeDtypeStruct((s, LANES), F32),
        scratch_shapes=[pltpu.VMEM((1, LANES), F32)],
        compiler_params=_cparams("arbitrary"),
        name="running_counts",
    )(info)


def _row_copy(src, dst, sem, src_row, dst_row):
    return pltpu.make_async_copy(src.at[pl.ds(src_row, 1)], dst.at[pl.ds(dst_row, 1)], sem)


def _dispatch_kernel(pos_ref, h_ref, init_ref, o_ref, sem):
    del init_ref
    tm = h_ref.shape[0]

    def issue(t, carry):
        for slot in range(TOP_K):
            _row_copy(h_ref, o_ref, sem, t, pos_ref[TOP_K * t + slot]).start()
        return carry

    lax.fori_loop(0, tm, issue, 0)

    def drain(t, carry):
        for slot in range(TOP_K):
            _row_copy(h_ref, o_ref, sem, t, pos_ref[TOP_K * t + slot]).wait()
        return carry

    lax.fori_loop(0, tm, drain, 0)


def _dispatch(h, pos_flat, rows_padded):
    s, d = h.shape
    tm = _pick(s, 256, 8)
    init = jnp.zeros((rows_padded, d), h.dtype)
    return pl.pallas_call(
        _dispatch_kernel,
        grid=(s // tm,),
        in_specs=[pl.BlockSpec((TOP_K * tm,), lambda i: (i,), memory_space=pltpu.SMEM),
                  pl.BlockSpec((tm, d), lambda i: (i, 0)),
                  pl.BlockSpec(memory_space=pl.ANY)],
        out_specs=pl.BlockSpec(memory_space=pl.ANY),
        out_shape=jax.ShapeDtypeStruct((rows_padded, d), h.dtype),
        scratch_shapes=[pltpu.SemaphoreType.DMA(())],
        input_output_aliases={2: 0},
        compiler_params=_cparams("arbitrary"),
        name="moe_dispatch",
    )(pos_flat, h, init)


def _moe_kernel(te_ref, used_ref, h_ref, wg_ref, wu_ref, wd_ref, o_ref, hb_ref):
    del te_ref
    i = pl.program_id(0)
    f = pl.program_id(1)
    active = i < used_ref[0]

    @pl.when(f == 0)
    def _():
        hb_ref[...] = h_ref[...].astype(BF16)
        o_ref[...] = jnp.zeros_like(o_ref)

    @pl.when(active)
    def _():
        h = hb_ref[...]
        a = jnp.dot(h, wg_ref[0], preferred_element_type=F32)
        b = jnp.dot(h, wu_ref[0], preferred_element_type=F32)
        o_ref[...] += jnp.dot((a * _sigmoid(a) * b).astype(BF16), wd_ref[0], preferred_element_type=F32)


def _moe_grouped(hs, tile_expert, tiles_used, wg, wu, wd, tm):
    rows, d = hs.shape
    f = wg.shape[2]
    tf = _pick(f, 512)
    grid_spec = pltpu.PrefetchScalarGridSpec(
        num_scalar_prefetch=2,
        grid=(rows // tm, f // tf),
        in_specs=[pl.BlockSpec((tm, d), lambda i, j, te, nu: (i, 0)),
                  pl.BlockSpec((1, d, tf), lambda i, j, te, nu: (te[i], 0, j)),
                  pl.BlockSpec((1, d, tf), lambda i, j, te, nu: (te[i], 0, j)),
                  pl.BlockSpec((1, tf, d), lambda i, j, te, nu: (te[i], j, 0))],
        out_specs=pl.BlockSpec((tm, d), lambda i, j, te, nu: (i, 0)),
        scratch_shapes=[pltpu.VMEM((tm, d), BF16)],
    )
    return pl.pallas_call(
        _moe_kernel,
        grid_spec=grid_spec,
        out_shape=jax.ShapeDtypeStruct((rows, d), F32),
        compiler_params=_cparams("arbitrary", "arbitrary"),
        name="moe_grouped",
    )(tile_expert, tiles_used, hs, wg, wu, wd)


def _combine_kernel(pos_ref, x_ref, info_ref, gpost_ref, mod_ref, gpre_ref, modn_ref, ys_ref,
                    xo_ref, *rest, experts, gate_row, shift_row, scale_row):
    *h_refs, buf_ref, sem = rest
    tm = x_ref.shape[0]

    def issue(t, carry):
        for slot in range(TOP_K):
            _row_copy(ys_ref, buf_ref.at[slot], sem, pos_ref[TOP_K * t + slot], t).start()
        return carry

    lax.fori_loop(0, tm, issue, 0)

    def drain(t, carry):
        for slot in range(TOP_K):
            _row_copy(ys_ref, buf_ref.at[slot], sem, pos_ref[TOP_K * t + slot], t).wait()
        return carry

    lax.fori_loop(0, tm, drain, 0)
    info = info_ref[...]
    w1 = info[:, experts + 2:experts + 3]
    w2 = info[:, experts + 3:experts + 4]
    y = w1 * buf_ref[0] + w2 * buf_ref[1]
    x = _resid_update(x_ref[...], y, gpost_ref[...], mod_ref[...], gate_row)
    xo_ref[...] = x
    if h_refs:
        h_ref, = h_refs
        h_ref[...] = _next_h(x, gpre_ref[...], modn_ref[...], shift_row, scale_row).astype(h_ref.dtype)


def _moe_combine_resid(x, ys, info, pos_flat, experts, gpost, mod, gate_row,
                       gpre=None, mod_next=None, shift_row=0, scale_row=0, h_dtype=None):
    s, d = x.shape
    tm = _pick(s, 256, 8)
    want_h = h_dtype is not None
    if not want_h:
        gpre, mod_next = gpost, mod
    row = pl.BlockSpec((tm, d), lambda i: (i, 0))
    vec = pl.BlockSpec((1, d), lambda i: (0, 0))
    tab = pl.BlockSpec((N_MOD, d), lambda i: (0, 0))
    out_shape = [jax.ShapeDtypeStruct((s, d), F32)]
    out_specs = [row]
    if want_h:
        out_shape.append(jax.ShapeDtypeStruct((s, d), h_dtype))
        out_specs.append(row)
    res = pl.pallas_call(
        functools.partial(_combine_kernel, experts=experts, gate_row=gate_row,
                          shift_row=shift_row, scale_row=scale_row),
        grid=(s // tm,),
        in_specs=[pl.BlockSpec((TOP_K * tm,), lambda i: (i,), memory_space=pltpu.SMEM),
                  row,
                  pl.BlockSpec((tm, LANES), lambda i: (i, 0)),
                  vec, tab, vec, tab,
                  pl.BlockSpec(memory_space=pl.ANY)],
        out_specs=out_specs,
        out_shape=out_shape,
        scratch_shapes=[pltpu.VMEM((TOP_K, tm, d), F32), pltpu.SemaphoreType.DMA(())],
        compiler_params=_cparams("arbitrary"),
        name="moe_combine",
    )(pos_flat, x, info, gpost.reshape(1, d), mod, gpre.reshape(1, d), mod_next, ys)
    return (res[0], res[1]) if want_h else (res[0], None)


def _dispatch_table(info, counts, experts, tm):
    s = info.shape[0]
    idx = info[:, experts:experts + TOP_K].astype(jnp.int32)
    rank = jnp.take_along_axis(counts[:, :experts], idx, axis=1).astype(jnp.int32) - 1
    total = counts[s - 1, :experts].astype(jnp.int32)
    padded = ((total + tm - 1) // tm) * tm
    ends = jnp.cumsum(padded)
    starts = ends - padded
    pos = (starts[idx] + rank).reshape(-1)
    n_tiles = (TOP_K * s) // tm + experts
    tile_start = jnp.arange(n_tiles, dtype=jnp.int32) * tm
    tile_expert = jnp.minimum(jnp.sum(tile_start[:, None] >= ends[None, :], axis=1), experts - 1)
    tiles_used = (ends[experts - 1] // tm).reshape(1)
    return pos.astype(jnp.int32), tile_expert.astype(jnp.int32), tiles_used.astype(jnp.int32), n_tiles * tm


def kernel(x, c, w_c, b_c, mod_table, pre_mix_norm, post_mix_norm, pre_ffn_norm, post_ffn_norm, w_in, w_out, conv_w, b_f, a_log, dt_bias, fox_norm, gdn_norm, w_gate_dense, w_up_dense, w_down_dense, w_router, w_gate_moe, w_up_moe, w_down_moe):
    batch, seq, d = x.shape
    assert batch == 1
    depth = mod_table.shape[0]
    half = d // 2
    heads = half // HEAD
    experts = w_router.shape[2]
    blk = _pick(seq, 512)

    o_ff = 3 * half
    o_g = o_ff + heads
    o_ga = o_g + 3 * half
    o_gz = o_ga + 2 * heads
    w_main = jnp.concatenate([w_in[:, :, :o_ff], w_in[:, :, o_g:o_ga], w_in[:, :, o_gz:]], axis=-1).astype(BF16)
    w_small = jnp.concatenate([w_in[:, :, o_ff:o_g], w_in[:, :, o_ga:o_gz],
                               jnp.zeros((depth, d, LANES - 3 * heads), F32)], axis=-1).astype(BF16)
    w_out_b = w_out.astype(BF16)

    mods = _mod_table(c, w_c, b_c, mod_table)
    xs = x.reshape(seq, d)
    moe_tm = _pick(seq, 256)

    h = _prenorm(xs, pre_mix_norm[0], mods[0], 0, 1, BF16)
    for l in range(depth):
        is_moe = l % 2 == 1
        j = l // 2
        proj = _matmul(h, w_main[l], BF16, name="in_proj")
        small = _matmul(h, w_small[l], F32, name="in_proj_small")
        p, pt = _gate_scalars(small, b_f[l], a_log[l], dt_bias[l])
        qa, ka, vt = _fox_prep(proj, p, heads, blk)
        fs = p[::blk, :heads].T
        o_fox = _fox_attention(qa, ka, vt, fs, fox_norm[l], blk)
        qkv = _gdn_prep(proj, conv_w[l], heads)
        o_gdn = _gdn(qkv, proj, pt, gdn_norm[l], heads)
        y = _out_proj(o_fox, o_gdn, w_out_b[l], BF16)
        xs, h = _resid(xs, y, post_mix_norm[l], mods[l], 2, pre_ffn_norm[l], mods[l], 3, 4,
                       F32 if is_moe else BF16)
        last = l == depth - 1
        nxt = dict(gpre=None if last else pre_mix_norm[l + 1], mod_next=None if last else mods[l + 1],
                   shift_row=0, scale_row=1, h_dtype=None if last else BF16)
        if not is_moe:
            y = _ffn_dense(h, w_gate_dense[j].astype(BF16), w_up_dense[j].astype(BF16),
                           w_down_dense[j].astype(BF16), BF16)
            xs, h = _resid(xs, y, post_ffn_norm[l], mods[l], 5, **nxt)
        else:
            info = _router(h, w_router[j])
            counts = _running_counts(info, experts)
            pos, tile_expert, tiles_used, rows_padded = _dispatch_table(info, counts, experts, moe_tm)
            hs = _dispatch(h, pos, rows_padded)
            ys = _moe_grouped(hs, tile_expert, tiles_used, w_gate_moe[j].astype(BF16),
                              w_up_moe[j].astype(BF16), w_down_moe[j].astype(BF16), moe_tm)
            xs, h = _moe_combine_resid(xs, ys, info, pos, experts, post_ffn_norm[l], mods[l], 5, **nxt)
    return xs.reshape(batch, seq, d)
```

```python
import functools

import jax
import jax.numpy as jnp
from jax import lax
from jax.experimental import pallas as pl
from jax.experimental.pallas import tpu as pltpu

F32 = jnp.float32
BF16 = jnp.bfloat16

HEAD = 128
LANES = 128
N_MOD = 6
EPS = 1e-6
CONV = 4
TOP_K = 2
LOG2E = 1.4426950408889634
NEG = -1e30
GDN_CHUNK = 128
VT_ROWS = HEAD + 16
GDN_GROUP = 8
NSUB = 4
VMEM_LIMIT_BYTES = 56 * 1024 * 1024


def _pick(n, pref, mult=LANES):
    if n <= pref:
        return n
    t = (pref // mult) * mult
    while t >= mult:
        if n % t == 0:
            return t
        t -= mult
    return n


def _cparams(*sem):
    return pltpu.CompilerParams(dimension_semantics=sem, vmem_limit_bytes=VMEM_LIMIT_BYTES)


def _rms(x):
    return x * lax.rsqrt(jnp.mean(x * x, axis=-1, keepdims=True) + EPS)


def _sigmoid(x):
    return 1.0 / (1.0 + jnp.exp(-x))


def _mod_kernel(c_ref, w_ref, b_ref, tab_ref, o_ref):
    c = c_ref[...]
    sc = c * _sigmoid(c)
    lhs = jnp.broadcast_to(sc, (8, sc.shape[1])).astype(BF16)
    base = jnp.dot(lhs, w_ref[...].astype(BF16), preferred_element_type=F32)[0:1]
    o_ref[...] = base + b_ref[...] + tab_ref[...]


def _mod_table(c, w_c, b_c, mod_table):
    depth = mod_table.shape[0]
    d = c.shape[1]
    n = w_c.shape[1]
    tn = _pick(n, 512)
    out = pl.pallas_call(
        _mod_kernel,
        grid=(n // tn,),
        in_specs=[pl.BlockSpec((1, d), lambda j: (0, 0)),
                  pl.BlockSpec((d, tn), lambda j: (0, j)),
                  pl.BlockSpec((1, tn), lambda j: (0, j)),
                  pl.BlockSpec((depth, tn), lambda j: (0, j))],
        out_specs=pl.BlockSpec((depth, tn), lambda j: (0, j)),
        out_shape=jax.ShapeDtypeStruct((depth, n), F32),
        compiler_params=_cparams("arbitrary"),
        name="mod_table",
    )(c, w_c, b_c.reshape(1, n), mod_table.reshape(depth, n))
    return out.reshape(depth, N_MOD, d)


def _prenorm_kernel(x_ref, g_ref, mod_ref, h_ref, *, shift_row, scale_row):
    h = _rms(x_ref[...]) * g_ref[...]
    h = h * (1.0 + mod_ref[scale_row:scale_row + 1, :]) + mod_ref[shift_row:shift_row + 1, :]
    h_ref[...] = h.astype(h_ref.dtype)


def _prenorm(x, gain, mod, shift_row, scale_row, out_dtype):
    s, d = x.shape
    tm = _pick(s, 256, 8)
    return pl.pallas_call(
        functools.partial(_prenorm_kernel, shift_row=shift_row, scale_row=scale_row),
        grid=(s // tm,),
        in_specs=[pl.BlockSpec((tm, d), lambda i: (i, 0)),
                  pl.BlockSpec((1, d), lambda i: (0, 0)),
                  pl.BlockSpec((N_MOD, d), lambda i: (0, 0))],
        out_specs=pl.BlockSpec((tm, d), lambda i: (i, 0)),
        out_shape=jax.ShapeDtypeStruct((s, d), out_dtype),
        compiler_params=_cparams("parallel"),
        name="prenorm",
    )(x, gain.reshape(1, d), mod)


def _resid_update(x, y, gpost, mod, gate_row):
    yn = _rms(y) * gpost
    return x + mod[gate_row:gate_row + 1, :] * yn


def _next_h(x, gpre, mod_next, shift_row, scale_row):
    h = _rms(x) * gpre
    return h * (1.0 + mod_next[scale_row:scale_row + 1, :]) + mod_next[shift_row:shift_row + 1, :]


def _resid_kernel(x_ref, y_ref, gpost_ref, mod_ref, gpre_ref, modn_ref, xo_ref, *h_refs,
                  gate_row, shift_row, scale_row):
    x = _resid_update(x_ref[...], y_ref[...].astype(F32), gpost_ref[...], mod_ref[...], gate_row)
    xo_ref[...] = x
    if h_refs:
        h_ref, = h_refs
        h_ref[...] = _next_h(x, gpre_ref[...], modn_ref[...], shift_row, scale_row).astype(h_ref.dtype)


def _resid(x, y, gpost, mod, gate_row, gpre=None, mod_next=None, shift_row=0, scale_row=0, h_dtype=None):
    s, d = x.shape
    tm = _pick(s, 256, 8)
    want_h = h_dtype is not None
    if not want_h:
        gpre, mod_next = gpost, mod
    row = pl.BlockSpec((tm, d), lambda i: (i, 0))
    vec = pl.BlockSpec((1, d), lambda i: (0, 0))
    tab = pl.BlockSpec((N_MOD, d), lambda i: (0, 0))
    out_shape = [jax.ShapeDtypeStruct((s, d), F32)]
    out_specs = [row]
    if want_h:
        out_shape.append(jax.ShapeDtypeStruct((s, d), h_dtype))
        out_specs.append(row)
    res = pl.pallas_call(
        functools.partial(_resid_kernel, gate_row=gate_row, shift_row=shift_row, scale_row=scale_row),
        grid=(s // tm,),
        in_specs=[row, row, vec, tab, vec, tab],
        out_specs=out_specs,
        out_shape=out_shape,
        compiler_params=_cparams("parallel"),
        name="resid",
    )(x, y, gpost.reshape(1, d), mod, gpre.reshape(1, d), mod_next)
    return (res[0], res[1]) if want_h else (res[0], None)


def _mm_kernel(a_ref, b_ref, o_ref):
    o_ref[...] = jnp.dot(a_ref[...], b_ref[...], preferred_element_type=F32).astype(o_ref.dtype)


def _matmul(a, b, layer, out_dtype, tm_pref=1024, tn_pref=1024, name="matmul"):
    m, k = a.shape
    n = b.shape[2]
    tm = _pick(m, tm_pref)
    tn = _pick(n, tn_pref)
    return pl.pallas_call(
        _mm_kernel,
        grid=(m // tm, n // tn),
        in_specs=[pl.BlockSpec((tm, k), lambda i, j: (i, 0)),
                  pl.BlockSpec((None, k, tn), lambda i, j: (layer, 0, j))],
        out_specs=pl.BlockSpec((tm, tn), lambda i, j: (i, j)),
        out_shape=jax.ShapeDtypeStruct((m, n), out_dtype),
        compiler_params=_cparams("parallel", "arbitrary"),
        name=name,
    )(a, b)


def _mm2_kernel(a1_ref, a2_ref, b1_ref, b2_ref, o_ref):
    acc = jnp.dot(a1_ref[...], b1_ref[...], preferred_element_type=F32)
    acc = acc + jnp.dot(a2_ref[...], b2_ref[...], preferred_element_type=F32)
    o_ref[...] = acc.astype(o_ref.dtype)


def _out_proj(a1, a2, w, layer, out_dtype):
    m, k1 = a1.shape
    k2 = a2.shape[1]
    assert k1 == k2 and w.shape[1] == k1 + k2
    n = w.shape[2]
    tm = _pick(m, 1024)
    tn = _pick(n, 1024)
    return pl.pallas_call(
        _mm2_kernel,
        grid=(m // tm, n // tn),
        in_specs=[pl.BlockSpec((tm, k1), lambda i, j: (i, 0)),
                  pl.BlockSpec((tm, k2), lambda i, j: (i, 0)),
                  pl.BlockSpec((None, k1, tn), lambda i, j: (layer, 0, j)),
                  pl.BlockSpec((None, k2, tn), lambda i, j: (layer, 1, j))],
        out_specs=pl.BlockSpec((tm, tn), lambda i, j: (i, j)),
        out_shape=jax.ShapeDtypeStruct((m, n), out_dtype),
        compiler_params=_cparams("parallel", "arbitrary"),
        name="out_proj",
    )(a1, a2, w, w)


def _scan_kernel(s_ref, bias_ref, alog_ref, o_ref, ot_ref, carry_ref, *, heads, rows):
    @pl.when(pl.program_id(0) == 0)
    def _():
        carry_ref[...] = jnp.zeros_like(carry_ref)

    h = heads
    lane = lax.broadcasted_iota(jnp.int32, (1, LANES), 1)
    ri = lax.broadcasted_iota(jnp.int32, (GDN_CHUNK, GDN_CHUNK), 0)
    ci = lax.broadcasted_iota(jnp.int32, (GDN_CHUNK, GDN_CHUNK), 1)
    tri = jnp.where(ri >= ci, 1.0, 0.0).astype(F32)
    a_exp = jnp.exp(alog_ref[...])
    for sb in range(rows // GDN_CHUNK):
        r0 = sb * GDN_CHUNK
        x = s_ref[r0:r0 + GDN_CHUNK, :] + bias_ref[...]
        soft = jnp.log(1.0 + jnp.exp(-jnp.abs(x)))
        log_sig = jnp.minimum(x, 0.0) - soft
        softplus = jnp.maximum(x, 0.0) + soft
        val = jnp.where(lane < h, log_sig * LOG2E, jnp.where(lane < 2 * h, -a_exp * softplus, 0.0))
        cs = jnp.dot(tri, val, preferred_element_type=F32, precision=lax.Precision.HIGHEST)
        run = cs + carry_ref[...]
        carry_ref[...] = run[GDN_CHUNK - 1:GDN_CHUNK, :]
        g_last = jnp.broadcast_to(cs[GDN_CHUNK - 1:GDN_CHUNK, :], cs.shape)
        g_last = pltpu.roll(g_last, 2 * h, axis=1)
        out = jnp.where(lane < h, run,
                        jnp.where(lane < 2 * h, cs,
                                  jnp.where(lane < 3 * h, _sigmoid(x),
                                            jnp.where(lane < 4 * h, g_last, 0.0))))
        o_ref[r0:r0 + GDN_CHUNK, :] = out
        ot_ref[:, r0:r0 + GDN_CHUNK] = out.T


def _gate_scalars(small, b_f, a_log, dt_bias):
    s = small.shape[0]
    h = b_f.shape[0]
    assert 4 * h <= LANES and s % GDN_CHUNK == 0
    rows = _pick(s, 512, GDN_CHUNK)
    pad = jnp.zeros((LANES - 2 * h,), F32)
    bias = jnp.concatenate([b_f, dt_bias, pad]).reshape(1, LANES)
    alog = jnp.concatenate([jnp.zeros((h,), F32), a_log, pad]).reshape(1, LANES)
    return pl.pallas_call(
        functools.partial(_scan_kernel, heads=h, rows=rows),
        grid=(s // rows,),
        in_specs=[pl.BlockSpec((rows, LANES), lambda i: (i, 0)),
                  pl.BlockSpec((1, LANES), lambda i: (0, 0)),
                  pl.BlockSpec((1, LANES), lambda i: (0, 0))],
        out_specs=[pl.BlockSpec((rows, LANES), lambda i: (i, 0)),
                   pl.BlockSpec((LANES, rows), lambda i: (0, i))],
        out_shape=[jax.ShapeDtypeStruct((s, LANES), F32),
                   jax.ShapeDtypeStruct((LANES, s), F32)],
        scratch_shapes=[pltpu.VMEM((1, LANES), F32)],
        compiler_params=_cparams("arbitrary"),
        name="gate_scalars",
    )(small, bias, alog)


def _fox_prep_kernel(q_ref, k_ref, v_ref, p_ref, qa_ref, ka_ref, vt_ref, *, heads):
    tm = q_ref.shape[0]
    scale = HEAD ** -0.5 * LOG2E
    lane = lax.broadcasted_iota(jnp.int32, (tm, HEAD), 1)
    row = lax.broadcasted_iota(jnp.int32, (HEAD, tm), 0)
    ones_rows = jnp.where(row < 3, 1.0, 0.0).astype(BF16)
    sum_row = jnp.where(lax.broadcasted_iota(jnp.int32, (VT_ROWS - HEAD, tm), 0) == 0, 1.0, 0.0)
    p = p_ref[...]
    for h in range(heads):
        sl = slice(h * HEAD, (h + 1) * HEAD)
        qa_ref[h, 0:HEAD, :] = (q_ref[:, sl].astype(F32) * scale).T.astype(BF16)
        qa_ref[h, HEAD:2 * HEAD, :] = ones_rows
        vt_ref[h, 0] = jnp.concatenate([v_ref[:, sl].astype(F32).T, sum_row], axis=0).astype(BF16)
        f_col = p[:, h:h + 1]
        e = f_col[0:1, :] - f_col
        e_hi = e.astype(BF16).astype(F32)
        e_mid = (e - e_hi).astype(BF16).astype(F32)
        e_lo = e - e_hi - e_mid
        cols = jnp.where(lane == 0, e_hi, jnp.where(lane == 1, e_mid, jnp.where(lane == 2, e_lo, 0.0)))
        ka_ref[h, 0, :, 0:HEAD] = k_ref[:, sl]
        ka_ref[h, 0, :, HEAD:2 * HEAD] = cols.astype(BF16)


def _fox_prep(proj, p, heads, blk):
    s = proj.shape[0]
    w = heads * HEAD
    return pl.pallas_call(
        functools.partial(_fox_prep_kernel, heads=heads),
        grid=(s // blk,),
        in_specs=[pl.BlockSpec((blk, w), lambda i: (i, 0)),
                  pl.BlockSpec((blk, w), lambda i: (i, 1)),
                  pl.BlockSpec((blk, w), lambda i: (i, 2)),
                  pl.BlockSpec((blk, LANES), lambda i: (i, 0))],
        out_specs=[pl.BlockSpec((heads, 2 * HEAD, blk), lambda i: (0, 0, i)),
                   pl.BlockSpec((heads, 1, blk, 2 * HEAD), lambda i: (0, i, 0, 0)),
                   pl.BlockSpec((heads, 1, VT_ROWS, blk), lambda i: (0, i, 0, 0))],
        out_shape=[jax.ShapeDtypeStruct((heads, 2 * HEAD, s), BF16),
                   jax.ShapeDtypeStruct((heads, s // blk, blk, 2 * HEAD), BF16),
                   jax.ShapeDtypeStruct((heads, s // blk, VT_ROWS, blk), BF16)],
        compiler_params=_cparams("parallel"),
        name="fox_prep",
    )(proj, proj, proj, p)


def _fox_kernel(fs_ref, qa_ref, ka_ref, vt_ref, gain_ref, o_ref, m_ref, acc_ref, *, blk, nsub):
    h = pl.program_id(0)
    i = pl.program_id(1)
    m_ref[...] = jnp.full_like(m_ref, NEG)
    acc_ref[...] = jnp.zeros_like(acc_ref)

    def steps(j, chains):
        ka = ka_ref[0, j]
        vt = vt_ref[0, j]
        f_k = fs_ref[h, j]
        scores = [jnp.dot(ka, qa_ref[0, :, a * blk:(a + 1) * blk], preferred_element_type=F32)
                  for a, _ in chains]
        probs, alphas = [], []
        for (a, diagonal), s in zip(chains, scores):
            if diagonal:
                kr = lax.broadcasted_iota(jnp.int32, s.shape, 0)
                qc = lax.broadcasted_iota(jnp.int32, s.shape, 1)
                s = jnp.where(kr <= qc, s, NEG)
            c = fs_ref[h, i * nsub + a] - f_k
            m_old = m_ref[a]
            m_new = jnp.maximum(m_old, jnp.max(s, axis=0, keepdims=True) + c)
            probs.append(jnp.exp2(s - (m_new - c)).astype(BF16))
            alphas.append(jnp.exp2(m_old - m_new))
            m_ref[a] = m_new
        for (a, _), p, alpha in zip(chains, probs, alphas):
            acc_ref[a] = alpha * acc_ref[a] + jnp.dot(vt, p, preferred_element_type=F32)

    def body(jj, carry):
        for t in range(nsub):
            steps(jj * nsub + t, [(a, False) for a in range(nsub)])
        return carry

    lax.fori_loop(0, i, body, 0)
    for t in range(nsub):
        steps(i * nsub + t, [(a, a == t) for a in range(t, nsub)])
    for a in range(nsub):
        acc = acc_ref[a]
        out = acc[0:HEAD] / acc[HEAD:HEAD + 1]
        out = out * lax.rsqrt(jnp.mean(out * out, axis=0, keepdims=True) + EPS)
        o_ref[a * blk:(a + 1) * blk, :] = (out.T * gain_ref[...]).astype(o_ref.dtype)


def _fox_attention(qa, ka, vt, fs, gain, blk):
    heads, _, s = qa.shape
    nk = s // blk
    nsub = NSUB if nk % NSUB == 0 else 1
    tq = nsub * blk
    vt_rows = vt.shape[2]
    grid_spec = pltpu.PrefetchScalarGridSpec(
        num_scalar_prefetch=1,
        grid=(heads, s // tq),
        in_specs=[pl.BlockSpec((1, 2 * HEAD, tq), lambda h, i, fs: (h, 0, i)),
                  pl.BlockSpec((1, nk, blk, 2 * HEAD), lambda h, i, fs: (h, 0, 0, 0)),
                  pl.BlockSpec((1, nk, vt_rows, blk), lambda h, i, fs: (h, 0, 0, 0)),
                  pl.BlockSpec((1, HEAD), lambda h, i, fs: (0, 0))],
        out_specs=pl.BlockSpec((tq, HEAD), lambda h, i, fs: (i, h)),
        scratch_shapes=[pltpu.VMEM((nsub, 1, blk), F32), pltpu.VMEM((nsub, vt_rows, blk), F32)],
    )
    return pl.pallas_call(
        functools.partial(_fox_kernel, blk=blk, nsub=nsub),
        grid_spec=grid_spec,
        out_shape=jax.ShapeDtypeStruct((s, heads * HEAD), BF16),
        compiler_params=_cparams("parallel", "arbitrary"),
        name="fox_attention",
    )(fs, qa, ka, vt, gain.reshape(1, HEAD))


def _gdn_prep_kernel(x_ref, prev_ref, w_ref, o_ref, buf_ref, *, heads):
    tm = x_ref.shape[0]
    halo = prev_ref.shape[0]
    prev = prev_ref[...].astype(F32)
    buf_ref[0:halo, :] = jnp.where(pl.program_id(0) == 0, 0.0, prev)
    buf_ref[halo:halo + tm, :] = x_ref[...].astype(F32)
    y = jnp.zeros(x_ref.shape, F32)
    for s in range(CONV):
        y = y + buf_ref[halo - s:halo - s + tm, :] * w_ref[CONV - 1 - s:CONV - s, :]
    y = y * _sigmoid(y)
    w = heads * HEAD
    for h in range(2 * heads):
        sl = slice(h * HEAD, (h + 1) * HEAD)
        v = y[:, sl]
        n = v * lax.rsqrt(jnp.sum(v * v, axis=-1, keepdims=True) + EPS)
        if h < heads:
            n = n * HEAD ** -0.5
        o_ref[:, sl] = n.astype(o_ref.dtype)
    o_ref[:, 2 * w:3 * w] = y[:, 2 * w:3 * w].astype(o_ref.dtype)


def _gdn_prep(proj, conv_w, heads):
    s = proj.shape[0]
    w3 = 3 * heads * HEAD
    tm = _pick(s, 256, 16)
    halo = 16
    per = tm // halo
    return pl.pallas_call(
        functools.partial(_gdn_prep_kernel, heads=heads),
        grid=(s // tm,),
        in_specs=[pl.BlockSpec((tm, w3), lambda i: (i, 1)),
                  pl.BlockSpec((halo, w3), lambda i: (jnp.maximum(i * per - 1, 0), 1)),
                  pl.BlockSpec((CONV, w3), lambda i: (0, 0))],
        out_specs=pl.BlockSpec((tm, w3), lambda i: (i, 0)),
        out_shape=jax.ShapeDtypeStruct((s, w3), BF16),
        scratch_shapes=[pltpu.VMEM((tm + halo, w3), F32)],
        compiler_params=_cparams("parallel"),
        name="gdn_prep",
    )(proj, proj, conv_w)


def _bdot(a, b):
    return jnp.dot(a.astype(BF16), b.astype(BF16), preferred_element_type=F32)


def _bdot_nt(a, b):
    return lax.dot_general(a.astype(BF16), b.astype(BF16), (((1,), (1,)), ((), ())),
                           preferred_element_type=F32)


def _gdn_kernel(q_ref, k_ref, v_ref, z_ref, pt_ref, gain_ref, o_ref, state_ref, *, heads, group):
    hg = pl.program_id(0)

    @pl.when(pl.program_id(1) == 0)
    def _():
        state_ref[...] = jnp.zeros_like(state_ref)

    tb = q_ref.shape[0]
    c = GDN_CHUNK
    ri = lax.broadcasted_iota(jnp.int32, (tb, tb), 0)
    ci = lax.broadcasted_iota(jnp.int32, (tb, tb), 1)
    rc_xor = jnp.bitwise_xor(ri, ci)
    same_chunk = (rc_xor >> (c.bit_length() - 1)) == 0
    causal = same_chunk & (ri >= ci)
    eye = jnp.where(ri == ci, 1.0, 0.0).astype(F32)

    def col(r):
        return jnp.broadcast_to(r, (HEAD, tb)).T

    hs = range(group)
    sls = [slice(gi * HEAD, (gi + 1) * HEAD) for gi in hs]
    g_row = [pt_ref[pl.ds(heads + hg * group + gi, 1), :] for gi in hs]
    b_row = [pt_ref[pl.ds(2 * heads + hg * group + gi, 1), :] for gi in hs]
    gl_row = [pt_ref[pl.ds(3 * heads + hg * group + gi, 1), :] for gi in hs]
    g_col = [col(r) for r in g_row]
    b_col = [col(r) for r in b_row]
    e_g = [jnp.exp(x) for x in g_col]
    e_tail = [jnp.exp(col(gl_row[gi]) - g_col[gi]) for gi in hs]
    q = [q_ref[:, sl].astype(F32) for sl in sls]
    k = [k_ref[:, sl].astype(F32) for sl in sls]
    v = [v_ref[:, sl].astype(F32) for sl in sls]
    kb = [k[gi] * b_col[gi] for gi in hs]
    kk = [_bdot_nt(kb[gi], k[gi]) for gi in hs]
    qk = [_bdot_nt(q[gi], k[gi]) for gi in hs]
    decay = [jnp.exp(jnp.where(causal, g_col[gi][:, 0:1] - g_row[gi], NEG)) for gi in hs]
    lower = [jnp.where(ri > ci, kk[gi] * decay[gi], 0.0) for gi in hs]
    attn = [qk[gi] * decay[gi] for gi in hs]
    inv = [eye - jnp.where(rc_xor == 1, lower[gi], 0.0) for gi in hs]
    b = 2
    while b < c:
        join = (rc_xor >> (b.bit_length() - 1)) == 1
        t = [_bdot(jnp.where(join, lower[gi], 0.0), inv[gi]) for gi in hs]
        inv = [inv[gi] - _bdot(inv[gi], t[gi]) for gi in hs]
        b *= 2
    uw = [_bdot(inv[gi], jnp.concatenate([v[gi] * b_col[gi], kb[gi] * e_g[gi]], axis=1)) for gi in hs]
    qg = [q[gi] * e_g[gi] for gi in hs]
    kt_t = [(k[gi] * e_tail[gi]).T for gi in hs]
    state = [state_ref[gi] for gi in hs]
    outs = [[] for _ in hs]
    for ch in range(tb // c):
        r = slice(ch * c, (ch + 1) * c)
        x = [_bdot(jnp.concatenate([uw[gi][r, HEAD:2 * HEAD], qg[gi][r]], axis=0), state[gi]) for gi in hs]
        v_new = [uw[gi][r, 0:HEAD] - x[gi][0:c] for gi in hs]
        for gi in hs:
            outs[gi].append(x[gi][c:2 * c] + _bdot(attn[gi][r, r], v_new[gi]))
        state = [state[gi] * jnp.exp(gl_row[gi][:, ch * c:ch * c + 1]) + _bdot(kt_t[gi][:, r], v_new[gi]) for gi in hs]
    for gi in hs:
        state_ref[gi] = state[gi]
        o = jnp.concatenate(outs[gi], axis=0)
        z = z_ref[:, sls[gi]].astype(F32)
        o = _rms(o) * gain_ref[...] * (z * _sigmoid(z))
        o_ref[:, sls[gi]] = o.astype(o_ref.dtype)


def _gdn(qkv, proj, pt, gain, heads, group=GDN_GROUP):
    s = qkv.shape[0]
    tb = _pick(s, 256, GDN_CHUNK)
    group = min(group, heads)
    gw = group * HEAD
    per = heads // group
    return pl.pallas_call(
        functools.partial(_gdn_kernel, heads=heads, group=group),
        grid=(per, s // tb),
        in_specs=[pl.BlockSpec((tb, gw), lambda g, i: (i, g)),
                  pl.BlockSpec((tb, gw), lambda g, i: (i, per + g)),
                  pl.BlockSpec((tb, gw), lambda g, i: (i, 2 * per + g)),
                  pl.BlockSpec((tb, gw), lambda g, i: (i, 6 * per + g)),
                  pl.BlockSpec((LANES, tb), lambda g, i: (0, i)),
                  pl.BlockSpec((1, HEAD), lambda g, i: (0, 0))],
        out_specs=pl.BlockSpec((tb, gw), lambda g, i: (i, g)),
        out_shape=jax.ShapeDtypeStruct((s, heads * HEAD), BF16),
        scratch_shapes=[pltpu.VMEM((group, HEAD, HEAD), F32)],
        compiler_params=_cparams("parallel", "arbitrary"),
        name="gdn",
    )(qkv, qkv, qkv, proj, pt, gain.reshape(1, HEAD))


def _ffn_kernel(h_ref, wg_ref, wu_ref, wd_ref, o_ref, acc_ref):
    f = pl.program_id(1)

    @pl.when(f == 0)
    def _():
        acc_ref[...] = jnp.zeros_like(acc_ref)

    h = h_ref[...]
    a = jnp.dot(h, wg_ref[...], preferred_element_type=F32)
    b = jnp.dot(h, wu_ref[...], preferred_element_type=F32)
    acc_ref[...] += jnp.dot((a * _sigmoid(a) * b).astype(BF16), wd_ref[...], preferred_element_type=F32)

    @pl.when(f == pl.num_programs(1) - 1)
    def _():
        o_ref[...] = acc_ref[...].astype(o_ref.dtype)


def _ffn_dense(h, wg, wu, wd, layer, out_dtype):
    s, d = h.shape
    f = wg.shape[2]
    tm = _pick(s, 512)
    tf = _pick(f, 256)
    return pl.pallas_call(
        _ffn_kernel,
        grid=(s // tm, f // tf),
        in_specs=[pl.BlockSpec((tm, d), lambda i, j: (i, 0)),
                  pl.BlockSpec((None, d, tf), lambda i, j: (layer, 0, j)),
                  pl.BlockSpec((None, d, tf), lambda i, j: (layer, 0, j)),
                  pl.BlockSpec((None, tf, d), lambda i, j: (layer, j, 0))],
        out_specs=pl.BlockSpec((tm, d), lambda i, j: (i, 0)),
        out_shape=jax.ShapeDtypeStruct((s, d), out_dtype),
        scratch_shapes=[pltpu.VMEM((tm, d), F32)],
        compiler_params=_cparams("parallel", "arbitrary"),
        name="ffn_dense",
    )(h, wg, wu, wd)


def _router_kernel(h_ref, w_ref, o_ref, *, experts):
    logits = jnp.dot(h_ref[...].astype(BF16), w_ref[...], preferred_element_type=F32)
    lane = lax.broadcasted_iota(jnp.int32, logits.shape, 1).astype(F32)
    logits = jnp.where(lane < experts, logits, -jnp.inf)
    m1 = jnp.max(logits, axis=-1, keepdims=True)
    i1 = jnp.min(jnp.where(logits == m1, lane, LANES), axis=-1, keepdims=True)
    rest = jnp.where(lane == i1, -jnp.inf, logits)
    m2 = jnp.max(rest, axis=-1, keepdims=True)
    i2 = jnp.min(jnp.where(rest == m2, lane, LANES), axis=-1, keepdims=True)
    e2 = jnp.exp(m2 - m1)
    w1 = 1.0 / (1.0 + e2)
    w2 = e2 / (1.0 + e2)
    hit = jnp.where((lane == i1) | (lane == i2), 1.0, 0.0)
    info = jnp.where(lane == experts, i1,
                     jnp.where(lane == experts + 1, i2,
                               jnp.where(lane == experts + 2, w1,
                                         jnp.where(lane == experts + 3, w2, hit))))
    o_ref[...] = info


def _router(h, w_router):
    s, d = h.shape
    e = w_router.shape[1]
    assert e + 4 <= LANES
    wr = jnp.zeros((d, LANES), BF16).at[:, :e].set(w_router.astype(BF16))
    tm = _pick(s, 256, 8)
    return pl.pallas_call(
        functools.partial(_router_kernel, experts=e),
        grid=(s // tm,),
        in_specs=[pl.BlockSpec((tm, d), lambda i: (i, 0)),
                  pl.BlockSpec((d, LANES), lambda i: (0, 0))],
        out_specs=pl.BlockSpec((tm, LANES), lambda i: (i, 0)),
        out_shape=jax.ShapeDtypeStruct((s, LANES), F32),
        compiler_params=_cparams("parallel"),
        name="router",
    )(h, wr)


def _count_kernel(r_ref, o_ref, carry_ref, *, experts):
    @pl.when(pl.program_id(0) == 0)
    def _():
        carry_ref[...] = jnp.zeros_like(carry_ref)

    rows = r_ref.shape[0]
    lane = lax.broadcasted_iota(jnp.int32, (1, LANES), 1)
    ri = lax.broadcasted_iota(jnp.int32, (rows, rows), 0)
    ci = lax.broadcasted_iota(jnp.int32, (rows, rows), 1)
    tri = jnp.where(ri >= ci, 1.0, 0.0).astype(BF16)
    hit = jnp.where(lane < experts, r_ref[...], 0.0)
    run = jnp.dot(tri, hit.astype(BF16), preferred_element_type=F32) + carry_ref[...]
    carry_ref[...] = run[rows - 1:rows, :]
    o_ref[...] = run


def _running_counts(info, experts):
    s = info.shape[0]
    rows = _pick(s, 256, 8)
    return pl.pallas_call(
        functools.partial(_count_kernel, experts=experts),
        grid=(s // rows,),
        in_specs=[pl.BlockSpec((rows, LANES), lambda i: (i, 0))],
        out_specs=pl.BlockSpec((rows, LANES), lambda i: (i, 0)),
        out_shape=jax.ShapeDtypeStruct((s, LANES), F32),
        scratch_shapes=[pltpu.VMEM((1, LANES), F32)],
        compiler_params=_cparams("arbitrary"),
        name="running_counts",
    )(info)


def _row_copy(src, dst, sem, src_row, dst_row):
    return pltpu.make_async_copy(src.at[pl.ds(src_row, 1)], dst.at[pl.ds(dst_row, 1)], sem)


def _dispatch_kernel(pos_ref, h_ref, init_ref, o_ref, sem):
    del init_ref
    tm = h_ref.shape[0]

    def issue(t, carry):
        for slot in range(TOP_K):
            _row_copy(h_ref, o_ref, sem, t, pos_ref[TOP_K * t + slot]).start()
        return carry

    lax.fori_loop(0, tm, issue, 0)

    def drain(t, carry):
        for slot in range(TOP_K):
            _row_copy(h_ref, o_ref, sem, t, pos_ref[TOP_K * t + slot]).wait()
        return carry

    lax.fori_loop(0, tm, drain, 0)


def _dispatch(h, pos_flat, rows_padded):
    s, d = h.shape
    tm = _pick(s, 256, 8)
    init = jnp.zeros((rows_padded, d), h.dtype)
    return pl.pallas_call(
        _dispatch_kernel,
        grid=(s // tm,),
        in_specs=[pl.BlockSpec((TOP_K * tm,), lambda i: (i,), memory_space=pltpu.SMEM),
                  pl.BlockSpec((tm, d), lambda i: (i, 0)),
                  pl.BlockSpec(memory_space=pl.ANY)],
        out_specs=pl.BlockSpec(memory_space=pl.ANY),
        out_shape=jax.ShapeDtypeStruct((rows_padded, d), h.dtype),
        scratch_shapes=[pltpu.SemaphoreType.DMA(())],
        input_output_aliases={2: 0},
        compiler_params=_cparams("arbitrary"),
        name="moe_dispatch",
    )(pos_flat, h, init)


def _moe_kernel(te_ref, used_ref, h_ref, wg_ref, wu_ref, wd_ref, o_ref, hb_ref):
    del te_ref
    i = pl.program_id(0)
    f = pl.program_id(1)
    active = i < used_ref[0]

    @pl.when(f == 0)
    def _():
        hb_ref[...] = h_ref[...].astype(BF16)
        o_ref[...] = jnp.zeros_like(o_ref)

    @pl.when(active)
    def _():
        h = hb_ref[...]
        a = jnp.dot(h, wg_ref[...], preferred_element_type=F32)
        b = jnp.dot(h, wu_ref[...], preferred_element_type=F32)
        o_ref[...] += jnp.dot((a * _sigmoid(a) * b).astype(BF16), wd_ref[...], preferred_element_type=F32)


def _moe_grouped(hs, tile_expert, tiles_used, wg, wu, wd, layer, tm):
    rows, d = hs.shape
    f = wg.shape[3]
    tf = _pick(f, 512)
    grid_spec = pltpu.PrefetchScalarGridSpec(
        num_scalar_prefetch=2,
        grid=(rows // tm, f // tf),
        in_specs=[pl.BlockSpec((tm, d), lambda i, j, te, nu: (i, 0)),
                  pl.BlockSpec((None, None, d, tf), lambda i, j, te, nu: (layer, te[i], 0, j)),
                  pl.BlockSpec((None, None, d, tf), lambda i, j, te, nu: (layer, te[i], 0, j)),
                  pl.BlockSpec((None, None, tf, d), lambda i, j, te, nu: (layer, te[i], j, 0))],
        out_specs=pl.BlockSpec((tm, d), lambda i, j, te, nu: (i, 0)),
        scratch_shapes=[pltpu.VMEM((tm, d), BF16)],
    )
    return pl.pallas_call(
        _moe_kernel,
        grid_spec=grid_spec,
        out_shape=jax.ShapeDtypeStruct((rows, d), F32),
        compiler_params=_cparams("arbitrary", "arbitrary"),
        name="moe_grouped",
    )(tile_expert, tiles_used, hs, wg, wu, wd)


def _combine_kernel(pos_ref, x_ref, info_ref, gpost_ref, mod_ref, gpre_ref, modn_ref, ys_ref,
                    xo_ref, *rest, experts, gate_row, shift_row, scale_row):
    *h_refs, buf_ref, sem = rest
    tm = x_ref.shape[0]

    def issue(t, carry):
        for slot in range(TOP_K):
            _row_copy(ys_ref, buf_ref.at[slot], sem, pos_ref[TOP_K * t + slot], t).start()
        return carry

    lax.fori_loop(0, tm, issue, 0)

    def drain(t, carry):
        for slot in range(TOP_K):
            _row_copy(ys_ref, buf_ref.at[slot], sem, pos_ref[TOP_K * t + slot], t).wait()
        return carry

    lax.fori_loop(0, tm, drain, 0)
    info = info_ref[...]
    w1 = info[:, experts + 2:experts + 3]
    w2 = info[:, experts + 3:experts + 4]
    y = w1 * buf_ref[0] + w2 * buf_ref[1]
    x = _resid_update(x_ref[...], y, gpost_ref[...], mod_ref[...], gate_row)
    xo_ref[...] = x
    if h_refs:
        h_ref, = h_refs
        h_ref[...] = _next_h(x, gpre_ref[...], modn_ref[...], shift_row, scale_row).astype(h_ref.dtype)


def _moe_combine_resid(x, ys, info, pos_flat, experts, gpost, mod, gate_row,
                       gpre=None, mod_next=None, shift_row=0, scale_row=0, h_dtype=None):
    s, d = x.shape
    tm = _pick(s, 256, 8)
    want_h = h_dtype is not None
    if not want_h:
        gpre, mod_next = gpost, mod
    row = pl.BlockSpec((tm, d), lambda i: (i, 0))
    vec = pl.BlockSpec((1, d), lambda i: (0, 0))
    tab = pl.BlockSpec((N_MOD, d), lambda i: (0, 0))
    out_shape = [jax.ShapeDtypeStruct((s, d), F32)]
    out_specs = [row]
    if want_h:
        out_shape.append(jax.ShapeDtypeStruct((s, d), h_dtype))
        out_specs.append(row)
    res = pl.pallas_call(
        functools.partial(_combine_kernel, experts=experts, gate_row=gate_row,
                          shift_row=shift_row, scale_row=scale_row),
        grid=(s // tm,),
        in_specs=[pl.BlockSpec((TOP_K * tm,), lambda i: (i,), memory_space=pltpu.SMEM),
                  row,
                  pl.BlockSpec((tm, LANES), lambda i: (i, 0)),
                  vec, tab, vec, tab,
                  pl.BlockSpec(memory_space=pl.ANY)],
        out_specs=out_specs,
        out_shape=out_shape,
        scratch_shapes=[pltpu.VMEM((TOP_K, tm, d), F32), pltpu.SemaphoreType.DMA(())],
        compiler_params=_cparams("arbitrary"),
        name="moe_combine",
    )(pos_flat, x, info, gpost.reshape(1, d), mod, gpre.reshape(1, d), mod_next, ys)
    return (res[0], res[1]) if want_h else (res[0], None)


def _dispatch_table(info, counts, experts, tm):
    s = info.shape[0]
    idx = info[:, experts:experts + TOP_K].astype(jnp.int32)
    rank = jnp.take_along_axis(counts[:, :experts], idx, axis=1).astype(jnp.int32) - 1
    total = counts[s - 1, :experts].astype(jnp.int32)
    padded = ((total + tm - 1) // tm) * tm
    ends = jnp.cumsum(padded)
    starts = ends - padded
    pos = (starts[idx] + rank).reshape(-1)
    n_tiles = (TOP_K * s) // tm + experts
    tile_start = jnp.arange(n_tiles, dtype=jnp.int32) * tm
    tile_expert = jnp.minimum(jnp.sum(tile_start[:, None] >= ends[None, :], axis=1), experts - 1)
    tiles_used = (ends[experts - 1] // tm).reshape(1)
    return pos.astype(jnp.int32), tile_expert.astype(jnp.int32), tiles_used.astype(jnp.int32), n_tiles * tm


def kernel(x, c, w_c, b_c, mod_table, pre_mix_norm, post_mix_norm, pre_ffn_norm, post_ffn_norm, w_in, w_out, conv_w, b_f, a_log, dt_bias, fox_norm, gdn_norm, w_gate_dense, w_up_dense, w_down_dense, w_router, w_gate_moe, w_up_moe, w_down_moe):
    batch, seq, d = x.shape
    assert batch == 1
    depth = mod_table.shape[0]
    half = d // 2
    heads = half // HEAD
    experts = w_router.shape[2]
    blk = _pick(seq, 512)

    o_ff = 3 * half
    o_g = o_ff + heads
    o_ga = o_g + 3 * half
    o_gz = o_ga + 2 * heads
    w_main = jnp.concatenate([w_in[:, :, :o_ff], w_in[:, :, o_g:o_ga], w_in[:, :, o_gz:]], axis=-1).astype(BF16)
    w_small = jnp.concatenate([w_in[:, :, o_ff:o_g], w_in[:, :, o_ga:o_gz],
                               jnp.zeros((depth, d, LANES - 3 * heads), F32)], axis=-1).astype(BF16)
    w_out_b = w_out.astype(BF16)
    w_dense = [w.astype(BF16) for w in (w_gate_dense, w_up_dense, w_down_dense)]
    w_moe = [w.astype(BF16) for w in (w_gate_moe, w_up_moe, w_down_moe)]

    mods = _mod_table(c, w_c, b_c, mod_table)
    xs = x.reshape(seq, d)
    moe_tm = _pick(seq, 256)

    h = _prenorm(xs, pre_mix_norm[0], mods[0], 0, 1, BF16)
    for l in range(depth):
        is_moe = l % 2 == 1
        j = l // 2
        proj = _matmul(h, w_main, l, BF16, name="in_proj")
        small = _matmul(h, w_small, l, F32, name="in_proj_small")
        p, pt = _gate_scalars(small, b_f[l], a_log[l], dt_bias[l])
        qa, ka, vt = _fox_prep(proj, p, heads, blk)
        fs = p[::blk, :heads].T
        o_fox = _fox_attention(qa, ka, vt, fs, fox_norm[l], blk)
        qkv = _gdn_prep(proj, conv_w[l], heads)
        o_gdn = _gdn(qkv, proj, pt, gdn_norm[l], heads)
        y = _out_proj(o_fox, o_gdn, w_out_b, l, BF16)
        xs, h = _resid(xs, y, post_mix_norm[l], mods[l], 2, pre_ffn_norm[l], mods[l], 3, 4,
                       F32 if is_moe else BF16)
        last = l == depth - 1
        nxt = dict(gpre=None if last else pre_mix_norm[l + 1], mod_next=None if last else mods[l + 1],
                   shift_row=0, scale_row=1, h_dtype=None if last else BF16)
        if not is_moe:
            y = _ffn_dense(h, *w_dense, j, BF16)
            xs, h = _resid(xs, y, post_ffn_norm[l], mods[l], 5, **nxt)
        else:
            info = _router(h, w_router[j])
            counts = _running_counts(info, experts)
            pos, tile_expert, tiles_used, rows_padded = _dispatch_table(info, counts, experts, moe_tm)
            hs = _dispatch(h, pos, rows_padded)
            ys = _moe_grouped(hs, tile_expert, tiles_used, *w_moe, j, moe_tm)
            xs, h = _moe_combine_resid(xs, ys, info, pos, experts, post_ffn_norm[l], mods[l], 5, **nxt)
    return xs.reshape(batch, seq, d)
```

```python
import functools

import jax
import jax.numpy as jnp
from jax import lax
from jax.experimental import pallas as pl
from jax.experimental.pallas import tpu as pltpu

F32 = jnp.float32
BF16 = jnp.bfloat16

HEAD = 128
LANES = 128
N_MOD = 6
EPS = 1e-6
CONV = 4
TOP_K = 2
LOG2E = 1.4426950408889634
NEG = -1e30
GDN_CHUNK = 128
VT_ROWS = HEAD + 16
GDN_GROUP = 8
NSUB = 4
VMEM_LIMIT_BYTES = 56 * 1024 * 1024


def _pick(n, pref, mult=LANES):
    if n <= pref:
        return n
    t = (pref // mult) * mult
    while t >= mult:
        if n % t == 0:
            return t
        t -= mult
    return n


def _cparams(*sem):
    return pltpu.CompilerParams(dimension_semantics=sem, vmem_limit_bytes=VMEM_LIMIT_BYTES)


def _rms(x):
    return x * lax.rsqrt(jnp.mean(x * x, axis=-1, keepdims=True) + EPS)


def _sigmoid(x):
    return 1.0 / (1.0 + jnp.exp(-x))


def _mod_kernel(c_ref, w_ref, b_ref, tab_ref, o_ref):
    c = c_ref[...]
    sc = c * _sigmoid(c)
    lhs = jnp.broadcast_to(sc, (8, sc.shape[1])).astype(BF16)
    base = jnp.dot(lhs, w_ref[...].astype(BF16), preferred_element_type=F32)[0:1]
    o_ref[...] = base + b_ref[...] + tab_ref[...]


def _mod_table(c, w_c, b_c, mod_table):
    depth = mod_table.shape[0]
    d = c.shape[1]
    n = w_c.shape[1]
    tn = _pick(n, 512)
    out = pl.pallas_call(
        _mod_kernel,
        grid=(n // tn,),
        in_specs=[pl.BlockSpec((1, d), lambda j: (0, 0)),
                  pl.BlockSpec((d, tn), lambda j: (0, j)),
                  pl.BlockSpec((1, tn), lambda j: (0, j)),
                  pl.BlockSpec((depth, tn), lambda j: (0, j))],
        out_specs=pl.BlockSpec((depth, tn), lambda j: (0, j)),
        out_shape=jax.ShapeDtypeStruct((depth, n), F32),
        compiler_params=_cparams("arbitrary"),
        name="mod_table",
    )(c, w_c, b_c.reshape(1, n), mod_table.reshape(depth, n))
    return out.reshape(depth, N_MOD, d)


def _prenorm_kernel(x_ref, g_ref, mod_ref, h_ref, *, shift_row, scale_row):
    h = _rms(x_ref[...]) * g_ref[...]
    h = h * (1.0 + mod_ref[scale_row:scale_row + 1, :]) + mod_ref[shift_row:shift_row + 1, :]
    h_ref[...] = h.astype(h_ref.dtype)


def _prenorm(x, gain, mod, shift_row, scale_row, out_dtype):
    s, d = x.shape
    tm = _pick(s, 256, 8)
    return pl.pallas_call(
        functools.partial(_prenorm_kernel, shift_row=shift_row, scale_row=scale_row),
        grid=(s // tm,),
        in_specs=[pl.BlockSpec((tm, d), lambda i: (i, 0)),
                  pl.BlockSpec((1, d), lambda i: (0, 0)),
                  pl.BlockSpec((N_MOD, d), lambda i: (0, 0))],
        out_specs=pl.BlockSpec((tm, d), lambda i: (i, 0)),
        out_shape=jax.ShapeDtypeStruct((s, d), out_dtype),
        compiler_params=_cparams("parallel"),
        name="prenorm",
    )(x, gain.reshape(1, d), mod)


def _resid_update(x, y, gpost, mod, gate_row):
    yn = _rms(y) * gpost
    return x + mod[gate_row:gate_row + 1, :] * yn


def _next_h(x, gpre, mod_next, shift_row, scale_row):
    h = _rms(x) * gpre
    return h * (1.0 + mod_next[scale_row:scale_row + 1, :]) + mod_next[shift_row:shift_row + 1, :]


def _pack_pairs(x):
    half = x.shape[1] // 2
    lo = lax.bitcast_convert_type(x[:, :half].astype(BF16).astype(F32), jnp.uint32) >> 16
    hi = lax.bitcast_convert_type(x[:, half:].astype(BF16).astype(F32), jnp.uint32) & jnp.uint32(0xFFFF0000)
    return lo | hi


def _unpack_pairs(u):
    lo = lax.bitcast_convert_type(u << 16, F32)
    hi = lax.bitcast_convert_type(u & jnp.uint32(0xFFFF0000), F32)
    return lo, hi


def _store_h(h_ref, h):
    h_ref[...] = _pack_pairs(h) if h_ref.dtype == jnp.uint32 else h.astype(h_ref.dtype)


def _resid_kernel(x_ref, y_ref, gpost_ref, mod_ref, gpre_ref, modn_ref, xo_ref, *h_refs,
                  gate_row, shift_row, scale_row):
    x = _resid_update(x_ref[...], y_ref[...].astype(F32), gpost_ref[...], mod_ref[...], gate_row)
    xo_ref[...] = x
    if h_refs:
        h_ref, = h_refs
        _store_h(h_ref, _next_h(x, gpre_ref[...], modn_ref[...], shift_row, scale_row))


def _resid(x, y, gpost, mod, gate_row, gpre=None, mod_next=None, shift_row=0, scale_row=0, h_dtype=None):
    s, d = x.shape
    tm = _pick(s, 256, 8)
    want_h = h_dtype is not None
    if not want_h:
        gpre, mod_next = gpost, mod
    row = pl.BlockSpec((tm, d), lambda i: (i, 0))
    vec = pl.BlockSpec((1, d), lambda i: (0, 0))
    tab = pl.BlockSpec((N_MOD, d), lambda i: (0, 0))
    out_shape = [jax.ShapeDtypeStruct((s, d), F32)]
    out_specs = [row]
    if want_h:
        dh = d // 2 if h_dtype == jnp.uint32 else d
        out_shape.append(jax.ShapeDtypeStruct((s, dh), h_dtype))
        out_specs.append(pl.BlockSpec((tm, dh), lambda i: (i, 0)))
    res = pl.pallas_call(
        functools.partial(_resid_kernel, gate_row=gate_row, shift_row=shift_row, scale_row=scale_row),
        grid=(s // tm,),
        in_specs=[row, row, vec, tab, vec, tab],
        out_specs=out_specs,
        out_shape=out_shape,
        compiler_params=_cparams("parallel"),
        name="resid",
    )(x, y, gpost.reshape(1, d), mod, gpre.reshape(1, d), mod_next)
    return (res[0], res[1]) if want_h else (res[0], None)


def _mm_kernel(a_ref, b_ref, o_ref):
    o_ref[...] = jnp.dot(a_ref[...], b_ref[...], preferred_element_type=F32).astype(o_ref.dtype)


def _matmul(a, b, layer, out_dtype, tm_pref=1024, tn_pref=1024, name="matmul"):
    m, k = a.shape
    n = b.shape[2]
    tm = _pick(m, tm_pref)
    tn = _pick(n, tn_pref)
    return pl.pallas_call(
        _mm_kernel,
        grid=(m // tm, n // tn),
        in_specs=[pl.BlockSpec((tm, k), lambda i, j: (i, 0)),
                  pl.BlockSpec((None, k, tn), lambda i, j: (layer, 0, j))],
        out_specs=pl.BlockSpec((tm, tn), lambda i, j: (i, j)),
        out_shape=jax.ShapeDtypeStruct((m, n), out_dtype),
        compiler_params=_cparams("parallel", "arbitrary"),
        name=name,
    )(a, b)


def _mm2_kernel(a1_ref, a2_ref, b1_ref, b2_ref, o_ref):
    acc = jnp.dot(a1_ref[...], b1_ref[...], preferred_element_type=F32)
    acc = acc + jnp.dot(a2_ref[...], b2_ref[...], preferred_element_type=F32)
    o_ref[...] = acc.astype(o_ref.dtype)


def _out_proj(a1, a2, w, layer, out_dtype):
    m, k1 = a1.shape
    k2 = a2.shape[1]
    assert k1 == k2 and w.shape[1] == k1 + k2
    n = w.shape[2]
    tm = _pick(m, 1024)
    tn = _pick(n, 1024)
    return pl.pallas_call(
        _mm2_kernel,
        grid=(m // tm, n // tn),
        in_specs=[pl.BlockSpec((tm, k1), lambda i, j: (i, 0)),
                  pl.BlockSpec((tm, k2), lambda i, j: (i, 0)),
                  pl.BlockSpec((None, k1, tn), lambda i, j: (layer, 0, j)),
                  pl.BlockSpec((None, k2, tn), lambda i, j: (layer, 1, j))],
        out_specs=pl.BlockSpec((tm, tn), lambda i, j: (i, j)),
        out_shape=jax.ShapeDtypeStruct((m, n), out_dtype),
        compiler_params=_cparams("parallel", "arbitrary"),
        name="out_proj",
    )(a1, a2, w, w)


def _regroup_kernel(w_ref, main_ref, small_ref, *, wide, narrow):
    at = 0
    for lo, hi in wide:
        main_ref[:, at:at + hi - lo] = w_ref[:, lo:hi].astype(main_ref.dtype)
        at += hi - lo
    parts = [w_ref[:, lo:hi] for lo, hi in narrow]
    used = sum(hi - lo for lo, hi in narrow)
    parts.append(jnp.zeros((w_ref.shape[0], LANES - used), w_ref.dtype))
    small_ref[...] = jnp.concatenate(parts, axis=1).astype(small_ref.dtype)


def _regroup_w_in(w_in, wide, narrow):
    depth, d, n = w_in.shape
    n_main = sum(hi - lo for lo, hi in wide)
    tk = _pick(d, 128, 16)
    return pl.pallas_call(
        functools.partial(_regroup_kernel, wide=wide, narrow=narrow),
        grid=(depth, d // tk),
        in_specs=[pl.BlockSpec((None, tk, n), lambda l, i: (l, i, 0))],
        out_specs=[pl.BlockSpec((None, tk, n_main), lambda l, i: (l, i, 0)),
                   pl.BlockSpec((None, tk, LANES), lambda l, i: (l, i, 0))],
        out_shape=[jax.ShapeDtypeStruct((depth, d, n_main), BF16),
                   jax.ShapeDtypeStruct((depth, d, LANES), BF16)],
        compiler_params=_cparams("parallel", "parallel"),
        name="regroup_w_in",
    )(w_in)


def _scan_kernel(s_ref, bias_ref, alog_ref, o_ref, ot_ref, carry_ref, *, heads, rows):
    @pl.when(pl.program_id(0) == 0)
    def _():
        carry_ref[...] = jnp.zeros_like(carry_ref)

    h = heads
    lane = lax.broadcasted_iota(jnp.int32, (1, LANES), 1)
    ri = lax.broadcasted_iota(jnp.int32, (GDN_CHUNK, GDN_CHUNK), 0)
    ci = lax.broadcasted_iota(jnp.int32, (GDN_CHUNK, GDN_CHUNK), 1)
    tri = jnp.where(ri >= ci, 1.0, 0.0).astype(F32)
    a_exp = jnp.exp(alog_ref[...])
    for sb in range(rows // GDN_CHUNK):
        r0 = sb * GDN_CHUNK
        x = s_ref[r0:r0 + GDN_CHUNK, :] + bias_ref[...]
        soft = jnp.log(1.0 + jnp.exp(-jnp.abs(x)))
        log_sig = jnp.minimum(x, 0.0) - soft
        softplus = jnp.maximum(x, 0.0) + soft
        val = jnp.where(lane < h, log_sig * LOG2E, jnp.where(lane < 2 * h, -a_exp * softplus, 0.0))
        cs = jnp.dot(tri, val, preferred_element_type=F32, precision=lax.Precision.HIGHEST)
        run = cs + carry_ref[...]
        carry_ref[...] = run[GDN_CHUNK - 1:GDN_CHUNK, :]
        g_last = jnp.broadcast_to(cs[GDN_CHUNK - 1:GDN_CHUNK, :], cs.shape)
        g_last = pltpu.roll(g_last, 2 * h, axis=1)
        out = jnp.where(lane < h, run,
                        jnp.where(lane < 2 * h, cs,
                                  jnp.where(lane < 3 * h, _sigmoid(x),
                                            jnp.where(lane < 4 * h, g_last, 0.0))))
        o_ref[r0:r0 + GDN_CHUNK, :] = out
        ot_ref[:, r0:r0 + GDN_CHUNK] = out.T


def _gate_scalars(small, b_f, a_log, dt_bias):
    s = small.shape[0]
    h = b_f.shape[0]
    assert 4 * h <= LANES and s % GDN_CHUNK == 0
    rows = _pick(s, 512, GDN_CHUNK)
    pad = jnp.zeros((LANES - 2 * h,), F32)
    bias = jnp.concatenate([b_f, dt_bias, pad]).reshape(1, LANES)
    alog = jnp.concatenate([jnp.zeros((h,), F32), a_log, pad]).reshape(1, LANES)
    return pl.pallas_call(
        functools.partial(_scan_kernel, heads=h, rows=rows),
        grid=(s // rows,),
        in_specs=[pl.BlockSpec((rows, LANES), lambda i: (i, 0)),
                  pl.BlockSpec((1, LANES), lambda i: (0, 0)),
                  pl.BlockSpec((1, LANES), lambda i: (0, 0))],
        out_specs=[pl.BlockSpec((rows, LANES), lambda i: (i, 0)),
                   pl.BlockSpec((LANES, rows), lambda i: (0, i))],
        out_shape=[jax.ShapeDtypeStruct((s, LANES), F32),
                   jax.ShapeDtypeStruct((LANES, s), F32)],
        scratch_shapes=[pltpu.VMEM((1, LANES), F32)],
        compiler_params=_cparams("arbitrary"),
        name="gate_scalars",
    )(small, bias, alog)


def _fox_prep_kernel(q_ref, k_ref, v_ref, p_ref, qa_ref, ka_ref, vt_ref, *, heads):
    tm = q_ref.shape[0]
    scale = HEAD ** -0.5 * LOG2E
    lane = lax.broadcasted_iota(jnp.int32, (tm, HEAD), 1)
    row = lax.broadcasted_iota(jnp.int32, (HEAD, tm), 0)
    ones_rows = jnp.where(row < 3, 1.0, 0.0).astype(BF16)
    sum_row = jnp.where(lax.broadcasted_iota(jnp.int32, (VT_ROWS - HEAD, tm), 0) == 0, 1.0, 0.0)
    p = p_ref[...]
    for h in range(heads):
        sl = slice(h * HEAD, (h + 1) * HEAD)
        qa_ref[h, 0:HEAD, :] = (q_ref[:, sl].astype(F32) * scale).T.astype(BF16)
        qa_ref[h, HEAD:2 * HEAD, :] = ones_rows
        vt_ref[h, 0] = jnp.concatenate([v_ref[:, sl].astype(F32).T, sum_row], axis=0).astype(BF16)
        f_col = p[:, h:h + 1]
        e = f_col[0:1, :] - f_col
        e_hi = e.astype(BF16).astype(F32)
        e_mid = (e - e_hi).astype(BF16).astype(F32)
        e_lo = e - e_hi - e_mid
        cols = jnp.where(lane == 0, e_hi, jnp.where(lane == 1, e_mid, jnp.where(lane == 2, e_lo, 0.0)))
        ka_ref[h, 0, :, 0:HEAD] = k_ref[:, sl]
        ka_ref[h, 0, :, HEAD:2 * HEAD] = cols.astype(BF16)


def _fox_prep(proj, p, heads, blk):
    s = proj.shape[0]
    w = heads * HEAD
    return pl.pallas_call(
        functools.partial(_fox_prep_kernel, heads=heads),
        grid=(s // blk,),
        in_specs=[pl.BlockSpec((blk, w), lambda i: (i, 0)),
                  pl.BlockSpec((blk, w), lambda i: (i, 1)),
                  pl.BlockSpec((blk, w), lambda i: (i, 2)),
                  pl.BlockSpec((blk, LANES), lambda i: (i, 0))],
        out_specs=[pl.BlockSpec((heads, 2 * HEAD, blk), lambda i: (0, 0, i)),
                   pl.BlockSpec((heads, 1, blk, 2 * HEAD), lambda i: (0, i, 0, 0)),
                   pl.BlockSpec((heads, 1, VT_ROWS, blk), lambda i: (0, i, 0, 0))],
        out_shape=[jax.ShapeDtypeStruct((heads, 2 * HEAD, s), BF16),
                   jax.ShapeDtypeStruct((heads, s // blk, blk, 2 * HEAD), BF16),
                   jax.ShapeDtypeStruct((heads, s // blk, VT_ROWS, blk), BF16)],
        compiler_params=_cparams("parallel"),
        name="fox_prep",
    )(proj, proj, proj, p)


def _fox_kernel(fs_ref, qa_ref, ka_ref, vt_ref, gain_ref, o_ref, m_ref, acc_ref, *, blk, nsub):
    h = pl.program_id(0)
    i = pl.program_id(1)
    m_ref[...] = jnp.full_like(m_ref, NEG)
    acc_ref[...] = jnp.zeros_like(acc_ref)

    def steps(j, chains):
        ka = ka_ref[0, j]
        vt = vt_ref[0, j]
        f_k = fs_ref[h, j]
        scores = [jnp.dot(ka, qa_ref[0, :, a * blk:(a + 1) * blk], preferred_element_type=F32)
                  for a, _ in chains]
        probs, alphas = [], []
        for (a, diagonal), s in zip(chains, scores):
            if diagonal:
                kr = lax.broadcasted_iota(jnp.int32, s.shape, 0)
                qc = lax.broadcasted_iota(jnp.int32, s.shape, 1)
                s = jnp.where(kr <= qc, s, NEG)
            c = fs_ref[h, i * nsub + a] - f_k
            m_old = m_ref[a]
            m_new = jnp.maximum(m_old, jnp.max(s, axis=0, keepdims=True) + c)
            probs.append(jnp.exp2(s - (m_new - c)).astype(BF16))
            alphas.append(jnp.exp2(m_old - m_new))
            m_ref[a] = m_new
        for (a, _), p, alpha in zip(chains, probs, alphas):
            acc_ref[a] = alpha * acc_ref[a] + jnp.dot(vt, p, preferred_element_type=F32)

    def body(jj, carry):
        for t in range(nsub):
            steps(jj * nsub + t, [(a, False) for a in range(nsub)])
        return carry

    lax.fori_loop(0, i, body, 0)
    for t in range(nsub):
        steps(i * nsub + t, [(a, a == t) for a in range(t, nsub)])
    for a in range(nsub):
        acc = acc_ref[a]
        out = acc[0:HEAD] / acc[HEAD:HEAD + 1]
        out = out * lax.rsqrt(jnp.mean(out * out, axis=0, keepdims=True) + EPS)
        o_ref[a * blk:(a + 1) * blk, :] = (out.T * gain_ref[...]).astype(o_ref.dtype)


def _fox_attention(qa, ka, vt, fs, gain, blk):
    heads, _, s = qa.shape
    nk = s // blk
    nsub = NSUB if nk % NSUB == 0 else 1
    tq = nsub * blk
    vt_rows = vt.shape[2]
    grid_spec = pltpu.PrefetchScalarGridSpec(
        num_scalar_prefetch=1,
        grid=(heads, s // tq),
        in_specs=[pl.BlockSpec((1, 2 * HEAD, tq), lambda h, i, fs: (h, 0, i)),
                  pl.BlockSpec((1, nk, blk, 2 * HEAD), lambda h, i, fs: (h, 0, 0, 0)),
                  pl.BlockSpec((1, nk, vt_rows, blk), lambda h, i, fs: (h, 0, 0, 0)),
                  pl.BlockSpec((1, HEAD), lambda h, i, fs: (0, 0))],
        out_specs=pl.BlockSpec((tq, HEAD), lambda h, i, fs: (i, h)),
        scratch_shapes=[pltpu.VMEM((nsub, 1, blk), F32), pltpu.VMEM((nsub, vt_rows, blk), F32)],
    )
    return pl.pallas_call(
        functools.partial(_fox_kernel, blk=blk, nsub=nsub),
        grid_spec=grid_spec,
        out_shape=jax.ShapeDtypeStruct((s, heads * HEAD), BF16),
        compiler_params=_cparams("parallel", "arbitrary"),
        name="fox_attention",
    )(fs, qa, ka, vt, gain.reshape(1, HEAD))


def _gdn_prep_kernel(x_ref, prev_ref, w_ref, o_ref, buf_ref, *, heads):
    tm = x_ref.shape[0]
    halo = prev_ref.shape[0]
    prev = prev_ref[...].astype(F32)
    buf_ref[0:halo, :] = jnp.where(pl.program_id(0) == 0, 0.0, prev)
    buf_ref[halo:halo + tm, :] = x_ref[...].astype(F32)
    y = jnp.zeros(x_ref.shape, F32)
    for s in range(CONV):
        y = y + buf_ref[halo - s:halo - s + tm, :] * w_ref[CONV - 1 - s:CONV - s, :]
    y = y * _sigmoid(y)
    w = heads * HEAD
    for h in range(2 * heads):
        sl = slice(h * HEAD, (h + 1) * HEAD)
        v = y[:, sl]
        n = v * lax.rsqrt(jnp.sum(v * v, axis=-1, keepdims=True) + EPS)
        if h < heads:
            n = n * HEAD ** -0.5
        o_ref[:, sl] = n.astype(o_ref.dtype)
    o_ref[:, 2 * w:3 * w] = y[:, 2 * w:3 * w].astype(o_ref.dtype)


def _gdn_prep(proj, conv_w, heads):
    s = proj.shape[0]
    w3 = 3 * heads * HEAD
    tm = _pick(s, 256, 16)
    halo = 16
    per = tm // halo
    return pl.pallas_call(
        functools.partial(_gdn_prep_kernel, heads=heads),
        grid=(s // tm,),
        in_specs=[pl.BlockSpec((tm, w3), lambda i: (i, 1)),
                  pl.BlockSpec((halo, w3), lambda i: (jnp.maximum(i * per - 1, 0), 1)),
                  pl.BlockSpec((CONV, w3), lambda i: (0, 0))],
        out_specs=pl.BlockSpec((tm, w3), lambda i: (i, 0)),
        out_shape=jax.ShapeDtypeStruct((s, w3), BF16),
        scratch_shapes=[pltpu.VMEM((tm + halo, w3), F32)],
        compiler_params=_cparams("parallel"),
        name="gdn_prep",
    )(proj, proj, conv_w)


def _bdot(a, b):
    return jnp.dot(a.astype(BF16), b.astype(BF16), preferred_element_type=F32)


def _bdot_nt(a, b):
    return lax.dot_general(a.astype(BF16), b.astype(BF16), (((1,), (1,)), ((), ())),
                           preferred_element_type=F32)


def _gdn_kernel(q_ref, k_ref, v_ref, z_ref, pt_ref, gain_ref, o_ref, state_ref, *, heads, group):
    hg = pl.program_id(0)

    @pl.when(pl.program_id(1) == 0)
    def _():
        state_ref[...] = jnp.zeros_like(state_ref)

    tb = q_ref.shape[0]
    c = GDN_CHUNK
    ri = lax.broadcasted_iota(jnp.int32, (tb, tb), 0)
    ci = lax.broadcasted_iota(jnp.int32, (tb, tb), 1)
    rc_xor = jnp.bitwise_xor(ri, ci)
    same_chunk = (rc_xor >> (c.bit_length() - 1)) == 0
    causal = same_chunk & (ri >= ci)
    eye = jnp.where(ri == ci, 1.0, 0.0).astype(F32)

    def col(r):
        return jnp.broadcast_to(r, (HEAD, tb)).T

    hs = range(group)
    sls = [slice(gi * HEAD, (gi + 1) * HEAD) for gi in hs]
    g_row = [pt_ref[pl.ds(heads + hg * group + gi, 1), :] for gi in hs]
    b_row = [pt_ref[pl.ds(2 * heads + hg * group + gi, 1), :] for gi in hs]
    gl_row = [pt_ref[pl.ds(3 * heads + hg * group + gi, 1), :] for gi in hs]
    g_col = [col(r) for r in g_row]
    b_col = [col(r) for r in b_row]
    e_g = [jnp.exp(x) for x in g_col]
    e_tail = [jnp.exp(col(gl_row[gi]) - g_col[gi]) for gi in hs]
    q = [q_ref[:, sl].astype(F32) for sl in sls]
    k = [k_ref[:, sl].astype(F32) for sl in sls]
    v = [v_ref[:, sl].astype(F32) for sl in sls]
    kb = [k[gi] * b_col[gi] for gi in hs]
    kk = [_bdot_nt(kb[gi], k[gi]) for gi in hs]
    qk = [_bdot_nt(q[gi], k[gi]) for gi in hs]
    decay = [jnp.exp(jnp.where(causal, g_col[gi][:, 0:1] - g_row[gi], NEG)) for gi in hs]
    lower = [jnp.where(ri > ci, kk[gi] * decay[gi], 0.0) for gi in hs]
    attn = [qk[gi] * decay[gi] for gi in hs]
    inv = [eye - jnp.where(rc_xor == 1, lower[gi], 0.0) for gi in hs]
    b = 2
    while b < c:
        join = (rc_xor >> (b.bit_length() - 1)) == 1
        t = [_bdot(jnp.where(join, lower[gi], 0.0), inv[gi]) for gi in hs]
        inv = [inv[gi] - _bdot(inv[gi], t[gi]) for gi in hs]
        b *= 2
    uw = [_bdot(inv[gi], jnp.concatenate([v[gi] * b_col[gi], kb[gi] * e_g[gi]], axis=1)) for gi in hs]
    qg = [q[gi] * e_g[gi] for gi in hs]
    kt_t = [(k[gi] * e_tail[gi]).T for gi in hs]
    state = [state_ref[gi] for gi in hs]
    outs = [[] for _ in hs]
    for ch in range(tb // c):
        r = slice(ch * c, (ch + 1) * c)
        x = [_bdot(jnp.concatenate([uw[gi][r, HEAD:2 * HEAD], qg[gi][r]], axis=0), state[gi]) for gi in hs]
        v_new = [uw[gi][r, 0:HEAD] - x[gi][0:c] for gi in hs]
        for gi in hs:
            outs[gi].append(x[gi][c:2 * c] + _bdot(attn[gi][r, r], v_new[gi]))
        state = [state[gi] * jnp.exp(gl_row[gi][:, ch * c:ch * c + 1]) + _bdot(kt_t[gi][:, r], v_new[gi]) for gi in hs]
    for gi in hs:
        state_ref[gi] = state[gi]
        o = jnp.concatenate(outs[gi], axis=0)
        z = z_ref[:, sls[gi]].astype(F32)
        o = _rms(o) * gain_ref[...] * (z * _sigmoid(z))
        o_ref[:, sls[gi]] = o.astype(o_ref.dtype)


def _gdn(qkv, proj, pt, gain, heads, group=GDN_GROUP):
    s = qkv.shape[0]
    tb = _pick(s, 256, GDN_CHUNK)
    group = min(group, heads)
    gw = group * HEAD
    per = heads // group
    return pl.pallas_call(
        functools.partial(_gdn_kernel, heads=heads, group=group),
        grid=(per, s // tb),
        in_specs=[pl.BlockSpec((tb, gw), lambda g, i: (i, g)),
                  pl.BlockSpec((tb, gw), lambda g, i: (i, per + g)),
                  pl.BlockSpec((tb, gw), lambda g, i: (i, 2 * per + g)),
                  pl.BlockSpec((tb, gw), lambda g, i: (i, 6 * per + g)),
                  pl.BlockSpec((LANES, tb), lambda g, i: (0, i)),
                  pl.BlockSpec((1, HEAD), lambda g, i: (0, 0))],
        out_specs=pl.BlockSpec((tb, gw), lambda g, i: (i, g)),
        out_shape=jax.ShapeDtypeStruct((s, heads * HEAD), BF16),
        scratch_shapes=[pltpu.VMEM((group, HEAD, HEAD), F32)],
        compiler_params=_cparams("parallel", "arbitrary"),
        name="gdn",
    )(qkv, qkv, qkv, proj, pt, gain.reshape(1, HEAD))


def _ffn_kernel(h_ref, wg_ref, wu_ref, wd_ref, o_ref, acc_ref):
    f = pl.program_id(1)

    @pl.when(f == 0)
    def _():
        acc_ref[...] = jnp.zeros_like(acc_ref)

    h = h_ref[...]
    a = jnp.dot(h, wg_ref[...], preferred_element_type=F32)
    b = jnp.dot(h, wu_ref[...], preferred_element_type=F32)
    acc_ref[...] += jnp.dot((a * _sigmoid(a) * b).astype(BF16), wd_ref[...], preferred_element_type=F32)

    @pl.when(f == pl.num_programs(1) - 1)
    def _():
        o_ref[...] = acc_ref[...].astype(o_ref.dtype)


def _ffn_dense(h, wg, wu, wd, layer, out_dtype):
    s, d = h.shape
    f = wg.shape[2]
    tm = _pick(s, 512)
    tf = _pick(f, 256)
    return pl.pallas_call(
        _ffn_kernel,
        grid=(s // tm, f // tf),
        in_specs=[pl.BlockSpec((tm, d), lambda i, j: (i, 0)),
                  pl.BlockSpec((None, d, tf), lambda i, j: (layer, 0, j)),
                  pl.BlockSpec((None, d, tf), lambda i, j: (layer, 0, j)),
                  pl.BlockSpec((None, tf, d), lambda i, j: (layer, j, 0))],
        out_specs=pl.BlockSpec((tm, d), lambda i, j: (i, 0)),
        out_shape=jax.ShapeDtypeStruct((s, d), out_dtype),
        scratch_shapes=[pltpu.VMEM((tm, d), F32)],
        compiler_params=_cparams("parallel", "arbitrary"),
        name="ffn_dense",
    )(h, wg, wu, wd)


def _router_kernel(h_ref, w_ref, o_ref, *, experts):
    lo, hi = _unpack_pairs(h_ref[...])
    half = lo.shape[1]
    logits = (jnp.dot(lo.astype(BF16), w_ref[0:half, :], preferred_element_type=F32)
              + jnp.dot(hi.astype(BF16), w_ref[half:2 * half, :], preferred_element_type=F32))
    lane = lax.broadcasted_iota(jnp.int32, logits.shape, 1).astype(F32)
    logits = jnp.where(lane < experts, logits, -jnp.inf)
    m1 = jnp.max(logits, axis=-1, keepdims=True)
    i1 = jnp.min(jnp.where(logits == m1, lane, LANES), axis=-1, keepdims=True)
    rest = jnp.where(lane == i1, -jnp.inf, logits)
    m2 = jnp.max(rest, axis=-1, keepdims=True)
    i2 = jnp.min(jnp.where(rest == m2, lane, LANES), axis=-1, keepdims=True)
    e2 = jnp.exp(m2 - m1)
    w1 = 1.0 / (1.0 + e2)
    w2 = e2 / (1.0 + e2)
    hit = jnp.where((lane == i1) | (lane == i2), 1.0, 0.0)
    info = jnp.where(lane == experts, i1,
                     jnp.where(lane == experts + 1, i2,
                               jnp.where(lane == experts + 2, w1,
                                         jnp.where(lane == experts + 3, w2, hit))))
    o_ref[...] = info


def _router(h, w_router):
    s, dh = h.shape
    d = 2 * dh
    e = w_router.shape[1]
    assert e + 4 <= LANES and w_router.shape[0] == d
    wr = jnp.zeros((d, LANES), BF16).at[:, :e].set(w_router.astype(BF16))
    tm = _pick(s, 256, 8)
    return pl.pallas_call(
        functools.partial(_router_kernel, experts=e),
        grid=(s // tm,),
        in_specs=[pl.BlockSpec((tm, dh), lambda i: (i, 0)),
                  pl.BlockSpec((d, LANES), lambda i: (0, 0))],
        out_specs=pl.BlockSpec((tm, LANES), lambda i: (i, 0)),
        out_shape=jax.ShapeDtypeStruct((s, LANES), F32),
        compiler_params=_cparams("parallel"),
        name="router",
    )(h, wr)


def _count_kernel(r_ref, o_ref, carry_ref, *, experts):
    @pl.when(pl.program_id(0) == 0)
    def _():
        carry_ref[...] = jnp.zeros_like(carry_ref)

    rows = r_ref.shape[0]
    lane = lax.broadcasted_iota(jnp.int32, (1, LANES), 1)
    ri = lax.broadcasted_iota(jnp.int32, (rows, rows), 0)
    ci = lax.broadcasted_iota(jnp.int32, (rows, rows), 1)
    tri = jnp.where(ri >= ci, 1.0, 0.0).astype(BF16)
    hit = jnp.where(lane < experts, r_ref[...], 0.0)
    run = jnp.dot(tri, hit.astype(BF16), preferred_element_type=F32) + carry_ref[...]
    carry_ref[...] = run[rows - 1:rows, :]
    o_ref[...] = run


def _running_counts(info, experts):
    s = info.shape[0]
    rows = _pick(s, 256, 8)
    return pl.pallas_call(
        functools.partial(_count_kernel, experts=experts),
        grid=(s // rows,),
        in_specs=[pl.BlockSpec((rows, LANES), lambda i: (i, 0))],
        out_specs=pl.BlockSpec((rows, LANES), lambda i: (i, 0)),
        out_shape=jax.ShapeDtypeStruct((s, LANES), F32),
        scratch_shapes=[pltpu.VMEM((1, LANES), F32)],
        compiler_params=_cparams("arbitrary"),
        name="running_counts",
    )(info)


def _row_copy(src, dst, sem, src_row, dst_row):
    return pltpu.make_async_copy(src.at[pl.ds(src_row, 1)], dst.at[pl.ds(dst_row, 1)], sem)


def _dispatch_kernel(pos_ref, h_ref, init_ref, o_ref, sem):
    del init_ref
    tm = h_ref.shape[0]

    def issue(t, carry):
        for slot in range(TOP_K):
            _row_copy(h_ref, o_ref, sem, t, pos_ref[TOP_K * t + slot]).start()
        return carry

    lax.fori_loop(0, tm, issue, 0)

    def drain(t, carry):
        for slot in range(TOP_K):
            _row_copy(h_ref, o_ref, sem, t, pos_ref[TOP_K * t + slot]).wait()
        return carry

    lax.fori_loop(0, tm, drain, 0)


def _dispatch(h, pos_flat, rows_padded):
    s, d = h.shape
    tm = _pick(s, 256, 8)
    init = jnp.zeros((rows_padded, d), h.dtype)
    return pl.pallas_call(
        _dispatch_kernel,
        grid=(s // tm,),
        in_specs=[pl.BlockSpec((TOP_K * tm,), lambda i: (i,), memory_space=pltpu.SMEM),
                  pl.BlockSpec((tm, d), lambda i: (i, 0)),
                  pl.BlockSpec(memory_space=pl.ANY)],
        out_specs=pl.BlockSpec(memory_space=pl.ANY),
        out_shape=jax.ShapeDtypeStruct((rows_padded, d), h.dtype),
        scratch_shapes=[pltpu.SemaphoreType.DMA(())],
        input_output_aliases={2: 0},
        compiler_params=_cparams("arbitrary"),
        name="moe_dispatch",
    )(pos_flat, h, init)


def _moe_kernel(te_ref, used_ref, h_ref, wg_ref, wu_ref, wd_ref, o_ref, hb_ref, acc_ref):
    del te_ref
    i = pl.program_id(0)
    f = pl.program_id(1)
    active = i < used_ref[0]
    half = h_ref.shape[1]

    @pl.when(f == 0)
    def _():
        lo, hi = _unpack_pairs(h_ref[...])
        hb_ref[:, 0:half] = lo.astype(BF16)
        hb_ref[:, half:2 * half] = hi.astype(BF16)
        acc_ref[...] = jnp.zeros_like(acc_ref)

    @pl.when(active)
    def _():
        h = hb_ref[...]
        a = jnp.dot(h, wg_ref[...], preferred_element_type=F32)
        b = jnp.dot(h, wu_ref[...], preferred_element_type=F32)
        acc_ref[...] += jnp.dot((a * _sigmoid(a) * b).astype(BF16), wd_ref[...], preferred_element_type=F32)

    @pl.when(f == pl.num_programs(1) - 1)
    def _():
        o_ref[...] = _pack_pairs(acc_ref[...])


def _moe_grouped(hs, tile_expert, tiles_used, wg, wu, wd, layer, tm):
    rows, dh = hs.shape
    d = 2 * dh
    f = wg.shape[3]
    tf = _pick(f, 256)
    grid_spec = pltpu.PrefetchScalarGridSpec(
        num_scalar_prefetch=2,
        grid=(rows // tm, f // tf),
        in_specs=[pl.BlockSpec((tm, dh), lambda i, j, te, nu: (i, 0)),
                  pl.BlockSpec((None, None, d, tf), lambda i, j, te, nu: (layer, te[i], 0, j)),
                  pl.BlockSpec((None, None, d, tf), lambda i, j, te, nu: (layer, te[i], 0, j)),
                  pl.BlockSpec((None, None, tf, d), lambda i, j, te, nu: (layer, te[i], j, 0))],
        out_specs=pl.BlockSpec((tm, dh), lambda i, j, te, nu: (i, 0)),
        scratch_shapes=[pltpu.VMEM((tm, d), BF16), pltpu.VMEM((tm, d), F32)],
    )
    return pl.pallas_call(
        _moe_kernel,
        grid_spec=grid_spec,
        out_shape=jax.ShapeDtypeStruct((rows, dh), jnp.uint32),
        compiler_params=_cparams("arbitrary", "arbitrary"),
        name="moe_grouped",
    )(tile_expert, tiles_used, hs, wg, wu, wd)


def _combine_kernel(pos_ref, x_ref, info_ref, gpost_ref, mod_ref, gpre_ref, modn_ref, ys_ref,
                    xo_ref, *rest, experts, gate_row, shift_row, scale_row):
    *h_refs, buf_ref, sem = rest
    tm = x_ref.shape[0]

    def issue(t, carry):
        for slot in range(TOP_K):
            _row_copy(ys_ref, buf_ref.at[slot], sem, pos_ref[TOP_K * t + slot], t).start()
        return carry

    lax.fori_loop(0, tm, issue, 0)

    def drain(t, carry):
        for slot in range(TOP_K):
            _row_copy(ys_ref, buf_ref.at[slot], sem, pos_ref[TOP_K * t + slot], t).wait()
        return carry

    lax.fori_loop(0, tm, drain, 0)
    info = info_ref[...]
    w1 = info[:, experts + 2:experts + 3]
    w2 = info[:, experts + 3:experts + 4]
    lo1, hi1 = _unpack_pairs(buf_ref[0])
    lo2, hi2 = _unpack_pairs(buf_ref[1])
    y = jnp.concatenate([w1 * lo1 + w2 * lo2, w1 * hi1 + w2 * hi2], axis=1)
    x = _resid_update(x_ref[...], y, gpost_ref[...], mod_ref[...], gate_row)
    xo_ref[...] = x
    if h_refs:
        h_ref, = h_refs
        _store_h(h_ref, _next_h(x, gpre_ref[...], modn_ref[...], shift_row, scale_row))


def _moe_combine_resid(x, ys, info, pos_flat, experts, gpost, mod, gate_row,
                       gpre=None, mod_next=None, shift_row=0, scale_row=0, h_dtype=None):
    s, d = x.shape
    tm = _pick(s, 256, 8)
    want_h = h_dtype is not None
    if not want_h:
        gpre, mod_next = gpost, mod
    row = pl.BlockSpec((tm, d), lambda i: (i, 0))
    vec = pl.BlockSpec((1, d), lambda i: (0, 0))
    tab = pl.BlockSpec((N_MOD, d), lambda i: (0, 0))
    out_shape = [jax.ShapeDtypeStruct((s, d), F32)]
    out_specs = [row]
    if want_h:
        out_shape.append(jax.ShapeDtypeStruct((s, d), h_dtype))
        out_specs.append(row)
    res = pl.pallas_call(
        functools.partial(_combine_kernel, experts=experts, gate_row=gate_row,
                          shift_row=shift_row, scale_row=scale_row),
        grid=(s // tm,),
        in_specs=[pl.BlockSpec((TOP_K * tm,), lambda i: (i,), memory_space=pltpu.SMEM),
                  row,
                  pl.BlockSpec((tm, LANES), lambda i: (i, 0)),
                  vec, tab, vec, tab,
                  pl.BlockSpec(memory_space=pl.ANY)],
        out_specs=out_specs,
        out_shape=out_shape,
        scratch_shapes=[pltpu.VMEM((TOP_K, tm, d // 2), jnp.uint32), pltpu.SemaphoreType.DMA(())],
        compiler_params=_cparams("arbitrary"),
        name="moe_combine",
    )(pos_flat, x, info, gpost.reshape(1, d), mod, gpre.reshape(1, d), mod_next, ys)
    return (res[0], res[1]) if want_h else (res[0], None)


def _dispatch_table(info, counts, experts, tm):
    s = info.shape[0]
    idx = info[:, experts:experts + TOP_K].astype(jnp.int32)
    rank = jnp.take_along_axis(counts[:, :experts], idx, axis=1).astype(jnp.int32) - 1
    total = counts[s - 1, :experts].astype(jnp.int32)
    padded = ((total + tm - 1) // tm) * tm
    ends = jnp.cumsum(padded)
    starts = ends - padded
    pos = (starts[idx] + rank).reshape(-1)
    n_tiles = (TOP_K * s) // tm + experts
    tile_start = jnp.arange(n_tiles, dtype=jnp.int32) * tm
    tile_expert = jnp.minimum(jnp.sum(tile_start[:, None] >= ends[None, :], axis=1), experts - 1)
    tiles_used = (ends[experts - 1] // tm).reshape(1)
    return pos.astype(jnp.int32), tile_expert.astype(jnp.int32), tiles_used.astype(jnp.int32), n_tiles * tm


def kernel(x, c, w_c, b_c, mod_table, pre_mix_norm, post_mix_norm, pre_ffn_norm, post_ffn_norm, w_in, w_out, conv_w, b_f, a_log, dt_bias, fox_norm, gdn_norm, w_gate_dense, w_up_dense, w_down_dense, w_router, w_gate_moe, w_up_moe, w_down_moe):
    batch, seq, d = x.shape
    assert batch == 1
    depth = mod_table.shape[0]
    half = d // 2
    heads = half // HEAD
    experts = w_router.shape[2]
    blk = _pick(seq, 512)

    o_ff = 3 * half
    o_g = o_ff + heads
    o_ga = o_g + 3 * half
    o_gz = o_ga + 2 * heads
    w_main, w_small = _regroup_w_in(w_in, ((0, o_ff), (o_g, o_ga), (o_gz, w_in.shape[2])), ((o_ff, o_g), (o_ga, o_gz)))
    w_out_b = w_out.astype(BF16)
    w_dense = [w.astype(BF16) for w in (w_gate_dense, w_up_dense, w_down_dense)]
    w_moe = [w.astype(BF16) for w in (w_gate_moe, w_up_moe, w_down_moe)]

    mods = _mod_table(c, w_c, b_c, mod_table)
    xs = x.reshape(seq, d)
    moe_tm = _pick(seq, 512)

    h = _prenorm(xs, pre_mix_norm[0], mods[0], 0, 1, BF16)
    for l in range(depth):
        is_moe = l % 2 == 1
        j = l // 2
        proj = _matmul(h, w_main, l, BF16, name="in_proj")
        small = _matmul(h, w_small, l, F32, name="in_proj_small")
        p, pt = _gate_scalars(small, b_f[l], a_log[l], dt_bias[l])
        qa, ka, vt = _fox_prep(proj, p, heads, blk)
        fs = p[::blk, :heads].T
        o_fox = _fox_attention(qa, ka, vt, fs, fox_norm[l], blk)
        qkv = _gdn_prep(proj, conv_w[l], heads)
        o_gdn = _gdn(qkv, proj, pt, gdn_norm[l], heads)
        y = _out_proj(o_fox, o_gdn, w_out_b, l, BF16)
        xs, h = _resid(xs, y, post_mix_norm[l], mods[l], 2, pre_ffn_norm[l], mods[l], 3, 4,
                       jnp.uint32 if is_moe else BF16)
        last = l == depth - 1
        nxt = dict(gpre=None if last else pre_mix_norm[l + 1], mod_next=None if last else mods[l + 1],
                   shift_row=0, scale_row=1, h_dtype=None if last else BF16)
        if not is_moe:
            y = _ffn_dense(h, *w_dense, j, BF16)
            xs, h = _resid(xs, y, post_ffn_norm[l], mods[l], 5, **nxt)
        else:
            info = _router(h, w_router[j])
            counts = _running_counts(info, experts)
            pos, tile_expert, tiles_used, rows_padded = _dispatch_table(info, counts, experts, moe_tm)
            hs = _dispatch(h, pos, rows_padded)
            ys = _moe_grouped(hs, tile_expert, tiles_used, *w_moe, j, moe_tm)
            xs, h = _moe_combine_resid(xs, ys, info, pos, experts, post_ffn_norm[l], mods[l], 5, **nxt)
    return xs.reshape(batch, seq, d)
```

```python
import functools

import jax
import jax.numpy as jnp
from jax import lax
from jax.experimental import pallas as pl
from jax.experimental.pallas import tpu as pltpu

F32 = jnp.float32
BF16 = jnp.bfloat16

HEAD = 128
LANES = 128
N_MOD = 6
EPS = 1e-6
CONV = 4
TOP_K = 2
LOG2E = 1.4426950408889634
NEG = -1e30
GDN_CHUNK = 128
VT_ROWS = HEAD + 16
GDN_GROUP = 8
NSUB = 4
VMEM_LIMIT_BYTES = 56 * 1024 * 1024


def _pick(n, pref, mult=LANES):
    if n <= pref:
        return n
    t = (pref // mult) * mult
    while t >= mult:
        if n % t == 0:
            return t
        t -= mult
    return n


def _cparams(*sem):
    return pltpu.CompilerParams(dimension_semantics=sem, vmem_limit_bytes=VMEM_LIMIT_BYTES)


def _rms(x):
    return x * lax.rsqrt(jnp.mean(x * x, axis=-1, keepdims=True) + EPS)


def _sigmoid(x):
    return 1.0 / (1.0 + jnp.exp(-x))


def _mod_kernel(c_ref, w_ref, b_ref, tab_ref, o_ref):
    c = c_ref[...]
    sc = c * _sigmoid(c)
    lhs = jnp.broadcast_to(sc, (8, sc.shape[1])).astype(BF16)
    base = jnp.dot(lhs, w_ref[...].astype(BF16), preferred_element_type=F32)[0:1]
    o_ref[...] = base + b_ref[...] + tab_ref[...]


def _mod_table(c, w_c, b_c, mod_table):
    depth = mod_table.shape[0]
    d = c.shape[1]
    n = w_c.shape[1]
    tn = _pick(n, 512)
    out = pl.pallas_call(
        _mod_kernel,
        grid=(n // tn,),
        in_specs=[pl.BlockSpec((1, d), lambda j: (0, 0)),
                  pl.BlockSpec((d, tn), lambda j: (0, j)),
                  pl.BlockSpec((1, tn), lambda j: (0, j)),
                  pl.BlockSpec((depth, tn), lambda j: (0, j))],
        out_specs=pl.BlockSpec((depth, tn), lambda j: (0, j)),
        out_shape=jax.ShapeDtypeStruct((depth, n), F32),
        compiler_params=_cparams("arbitrary"),
        name="mod_table",
    )(c, w_c, b_c.reshape(1, n), mod_table.reshape(depth, n))
    return out.reshape(depth, N_MOD, d)


def _prenorm_kernel(x_ref, g_ref, mod_ref, h_ref, *, shift_row, scale_row):
    h = _rms(x_ref[...]) * g_ref[...]
    h = h * (1.0 + mod_ref[scale_row:scale_row + 1, :]) + mod_ref[shift_row:shift_row + 1, :]
    h_ref[...] = h.astype(h_ref.dtype)


def _prenorm(x, gain, mod, shift_row, scale_row, out_dtype):
    s, d = x.shape
    tm = _pick(s, 256, 8)
    return pl.pallas_call(
        functools.partial(_prenorm_kernel, shift_row=shift_row, scale_row=scale_row),
        grid=(s // tm,),
        in_specs=[pl.BlockSpec((tm, d), lambda i: (i, 0)),
                  pl.BlockSpec((1, d), lambda i: (0, 0)),
                  pl.BlockSpec((N_MOD, d), lambda i: (0, 0))],
        out_specs=pl.BlockSpec((tm, d), lambda i: (i, 0)),
        out_shape=jax.ShapeDtypeStruct((s, d), out_dtype),
        compiler_params=_cparams("parallel"),
        name="prenorm",
    )(x, gain.reshape(1, d), mod)


def _resid_update(x, y, gpost, mod, gate_row):
    yn = _rms(y) * gpost
    return x + mod[gate_row:gate_row + 1, :] * yn


def _next_h(x, gpre, mod_next, shift_row, scale_row):
    h = _rms(x) * gpre
    return h * (1.0 + mod_next[scale_row:scale_row + 1, :]) + mod_next[shift_row:shift_row + 1, :]


def _pack_pairs(x):
    half = x.shape[1] // 2
    lo = lax.bitcast_convert_type(x[:, :half].astype(BF16).astype(F32), jnp.uint32) >> 16
    hi = lax.bitcast_convert_type(x[:, half:].astype(BF16).astype(F32), jnp.uint32) & jnp.uint32(0xFFFF0000)
    return lo | hi


def _unpack_pairs(u):
    lo = lax.bitcast_convert_type(u << 16, F32)
    hi = lax.bitcast_convert_type(u & jnp.uint32(0xFFFF0000), F32)
    return lo, hi


def _store_h(h_ref, h):
    h_ref[...] = _pack_pairs(h) if h_ref.dtype == jnp.uint32 else h.astype(h_ref.dtype)


def _resid_kernel(x_ref, y_ref, gpost_ref, mod_ref, gpre_ref, modn_ref, xo_ref, *h_refs,
                  gate_row, shift_row, scale_row):
    x = _resid_update(x_ref[...], y_ref[...].astype(F32), gpost_ref[...], mod_ref[...], gate_row)
    xo_ref[...] = x
    if h_refs:
        h_ref, = h_refs
        _store_h(h_ref, _next_h(x, gpre_ref[...], modn_ref[...], shift_row, scale_row))


def _resid(x, y, gpost, mod, gate_row, gpre=None, mod_next=None, shift_row=0, scale_row=0, h_dtype=None):
    s, d = x.shape
    tm = _pick(s, 256, 8)
    want_h = h_dtype is not None
    if not want_h:
        gpre, mod_next = gpost, mod
    row = pl.BlockSpec((tm, d), lambda i: (i, 0))
    vec = pl.BlockSpec((1, d), lambda i: (0, 0))
    tab = pl.BlockSpec((N_MOD, d), lambda i: (0, 0))
    out_shape = [jax.ShapeDtypeStruct((s, d), F32)]
    out_specs = [row]
    if want_h:
        dh = d // 2 if h_dtype == jnp.uint32 else d
        out_shape.append(jax.ShapeDtypeStruct((s, dh), h_dtype))
        out_specs.append(pl.BlockSpec((tm, dh), lambda i: (i, 0)))
    res = pl.pallas_call(
        functools.partial(_resid_kernel, gate_row=gate_row, shift_row=shift_row, scale_row=scale_row),
        grid=(s // tm,),
        in_specs=[row, row, vec, tab, vec, tab],
        out_specs=out_specs,
        out_shape=out_shape,
        compiler_params=_cparams("parallel"),
        name="resid",
    )(x, y, gpost.reshape(1, d), mod, gpre.reshape(1, d), mod_next)
    return (res[0], res[1]) if want_h else (res[0], None)


def _mm_kernel(a_ref, b_ref, o_ref):
    o_ref[...] = jnp.dot(a_ref[...], b_ref[...], preferred_element_type=F32).astype(o_ref.dtype)


def _matmul(a, b, layer, out_dtype, tm_pref=1024, tn_pref=1024, name="matmul"):
    m, k = a.shape
    n = b.shape[2]
    tm = _pick(m, tm_pref)
    tn = _pick(n, tn_pref)
    return pl.pallas_call(
        _mm_kernel,
        grid=(m // tm, n // tn),
        in_specs=[pl.BlockSpec((tm, k), lambda i, j: (i, 0)),
                  pl.BlockSpec((None, k, tn), lambda i, j: (layer, 0, j))],
        out_specs=pl.BlockSpec((tm, tn), lambda i, j: (i, j)),
        out_shape=jax.ShapeDtypeStruct((m, n), out_dtype),
        compiler_params=_cparams("parallel", "arbitrary"),
        name=name,
    )(a, b)


def _mm_nt_kernel(a_ref, bt_ref, o_ref):
    acc = lax.dot_general(a_ref[...], bt_ref[...], (((1,), (1,)), ((), ())), preferred_element_type=F32)
    o_ref[...] = acc.astype(o_ref.dtype)


def _matmul_nt(a, bt, layer, out_dtype, tm_pref=1024, tn_pref=1024, name="matmul_nt"):
    m, k = a.shape
    n = bt.shape[1]
    tm = _pick(m, tm_pref)
    tn = _pick(n, tn_pref)
    return pl.pallas_call(
        _mm_nt_kernel,
        grid=(m // tm, n // tn),
        in_specs=[pl.BlockSpec((tm, k), lambda i, j: (i, 0)),
                  pl.BlockSpec((None, tn, k), lambda i, j: (layer, j, 0))],
        out_specs=pl.BlockSpec((tm, tn), lambda i, j: (i, j)),
        out_shape=jax.ShapeDtypeStruct((m, n), out_dtype),
        compiler_params=_cparams("parallel", "arbitrary"),
        name=name,
    )(a, bt)


def _mm2_kernel(a1_ref, a2_ref, b1_ref, b2_ref, o_ref):
    acc = jnp.dot(a1_ref[...], b1_ref[...], preferred_element_type=F32)
    acc = acc + jnp.dot(a2_ref[...], b2_ref[...], preferred_element_type=F32)
    o_ref[...] = acc.astype(o_ref.dtype)


def _out_proj(a1, a2, w, layer, out_dtype):
    m, k1 = a1.shape
    k2 = a2.shape[1]
    assert k1 == k2 and w.shape[1] == k1 + k2
    n = w.shape[2]
    tm = _pick(m, 1024)
    tn = _pick(n, 1024)
    return pl.pallas_call(
        _mm2_kernel,
        grid=(m // tm, n // tn),
        in_specs=[pl.BlockSpec((tm, k1), lambda i, j: (i, 0)),
                  pl.BlockSpec((tm, k2), lambda i, j: (i, 0)),
                  pl.BlockSpec((None, k1, tn), lambda i, j: (layer, 0, j)),
                  pl.BlockSpec((None, k2, tn), lambda i, j: (layer, 1, j))],
        out_specs=pl.BlockSpec((tm, tn), lambda i, j: (i, j)),
        out_shape=jax.ShapeDtypeStruct((m, n), out_dtype),
        compiler_params=_cparams("parallel", "arbitrary"),
        name="out_proj",
    )(a1, a2, w, w)


def _regroup_kernel(w_ref, main_ref, small_ref, buf_ref, sbuf_ref, sem, ssem, *, wide, narrow, rows, blocks):
    l = pl.program_id(0)
    i = pl.program_id(1)
    step = l * blocks + i

    def src_copy(t, slot):
        tl = t // blocks
        ti = t - tl * blocks
        row = ti * rows
        at = 0
        for lo, hi in wide:
            row = row + jnp.where(ti * rows >= at, lo - at, 0) - jnp.where(ti * rows >= at + hi - lo, lo - at, 0)
            at += hi - lo
        return pltpu.make_async_copy(w_ref.at[tl, pl.ds(pl.multiple_of(row, 16), rows)], buf_ref.at[slot], sem.at[slot])

    @pl.when(step == 0)
    def _():
        src_copy(step, 0).start()

    @pl.when(step + 1 < pl.num_programs(0) * blocks)
    def _():
        src_copy(step + 1, (step + 1) % 2).start()

    @pl.when(i == 0)
    def _():
        sbuf_ref[...] = jnp.zeros_like(sbuf_ref)
        at = 0
        copies = []
        for lo, hi in narrow:
            copies.append(pltpu.make_async_copy(w_ref.at[l, lo:hi], sbuf_ref.at[at:at + hi - lo], ssem))
            at += hi - lo
        for cp in copies:
            cp.start()
        for cp in copies:
            cp.wait()
        small_ref[...] = sbuf_ref[...].astype(small_ref.dtype)

    src_copy(step, step % 2).wait()
    main_ref[...] = buf_ref[step % 2].astype(main_ref.dtype)


def _regroup_w_in(w_t, wide, narrow):
    depth, n, k = w_t.shape
    sizes = [hi - lo for lo, hi in wide]
    n_main = sum(sizes)
    rows = 512
    while any(sz % rows for sz in sizes):
        rows //= 2
    assert rows >= 16 and all(lo % 16 == 0 for lo, _ in wide + narrow)
    assert sum(hi - lo for lo, hi in narrow) <= LANES and all((hi - lo) % 16 == 0 for lo, hi in narrow)
    blocks = n_main // rows
    return pl.pallas_call(
        functools.partial(_regroup_kernel, wide=wide, narrow=narrow, rows=rows, blocks=blocks),
        grid=(depth, blocks),
        in_specs=[pl.BlockSpec(memory_space=pl.ANY)],
        out_specs=[pl.BlockSpec((None, rows, k), lambda l, i: (l, i, 0)),
                   pl.BlockSpec((None, LANES, k), lambda l, i: (l, 0, 0))],
        out_shape=[jax.ShapeDtypeStruct((depth, n_main, k), BF16),
                   jax.ShapeDtypeStruct((depth, LANES, k), BF16)],
        scratch_shapes=[pltpu.VMEM((2, rows, k), F32), pltpu.VMEM((LANES, k), F32),
                        pltpu.SemaphoreType.DMA((2,)), pltpu.SemaphoreType.DMA(())],
        compiler_params=_cparams("arbitrary", "arbitrary"),
        name="regroup_w_in",
    )(w_t)


def _scan_kernel(s_ref, bias_ref, alog_ref, o_ref, ot_ref, carry_ref, *, heads, rows):
    @pl.when(pl.program_id(0) == 0)
    def _():
        carry_ref[...] = jnp.zeros_like(carry_ref)

    h = heads
    lane = lax.broadcasted_iota(jnp.int32, (1, LANES), 1)
    ri = lax.broadcasted_iota(jnp.int32, (GDN_CHUNK, GDN_CHUNK), 0)
    ci = lax.broadcasted_iota(jnp.int32, (GDN_CHUNK, GDN_CHUNK), 1)
    tri = jnp.where(ri >= ci, 1.0, 0.0).astype(F32)
    a_exp = jnp.exp(alog_ref[...])
    for sb in range(rows // GDN_CHUNK):
        r0 = sb * GDN_CHUNK
        x = s_ref[r0:r0 + GDN_CHUNK, :] + bias_ref[...]
        soft = jnp.log(1.0 + jnp.exp(-jnp.abs(x)))
        log_sig = jnp.minimum(x, 0.0) - soft
        softplus = jnp.maximum(x, 0.0) + soft
        val = jnp.where(lane < h, log_sig * LOG2E, jnp.where(lane < 2 * h, -a_exp * softplus, 0.0))
        cs = jnp.dot(tri, val, preferred_element_type=F32, precision=lax.Precision.HIGHEST)
        run = cs + carry_ref[...]
        carry_ref[...] = run[GDN_CHUNK - 1:GDN_CHUNK, :]
        g_last = jnp.broadcast_to(cs[GDN_CHUNK - 1:GDN_CHUNK, :], cs.shape)
        g_last = pltpu.roll(g_last, 2 * h, axis=1)
        out = jnp.where(lane < h, run,
                        jnp.where(lane < 2 * h, cs,
                                  jnp.where(lane < 3 * h, _sigmoid(x),
                                            jnp.where(lane < 4 * h, g_last, 0.0))))
        o_ref[r0:r0 + GDN_CHUNK, :] = out
        ot_ref[:, r0:r0 + GDN_CHUNK] = out.T


def _gate_scalars(small, b_f, a_log, dt_bias):
    s = small.shape[0]
    h = b_f.shape[0]
    assert 4 * h <= LANES and s % GDN_CHUNK == 0
    rows = _pick(s, 512, GDN_CHUNK)
    pad = jnp.zeros((LANES - 2 * h,), F32)
    bias = jnp.concatenate([b_f, dt_bias, pad]).reshape(1, LANES)
    alog = jnp.concatenate([jnp.zeros((h,), F32), a_log, pad]).reshape(1, LANES)
    return pl.pallas_call(
        functools.partial(_scan_kernel, heads=h, rows=rows),
        grid=(s // rows,),
        in_specs=[pl.BlockSpec((rows, LANES), lambda i: (i, 0)),
                  pl.BlockSpec((1, LANES), lambda i: (0, 0)),
                  pl.BlockSpec((1, LANES), lambda i: (0, 0))],
        out_specs=[pl.BlockSpec((rows, LANES), lambda i: (i, 0)),
                   pl.BlockSpec((LANES, rows), lambda i: (0, i))],
        out_shape=[jax.ShapeDtypeStruct((s, LANES), F32),
                   jax.ShapeDtypeStruct((LANES, s), F32)],
        scratch_shapes=[pltpu.VMEM((1, LANES), F32)],
        compiler_params=_cparams("arbitrary"),
        name="gate_scalars",
    )(small, bias, alog)


def _fox_prep_kernel(q_ref, k_ref, v_ref, p_ref, qa_ref, ka_ref, vt_ref, *, heads):
    tm = q_ref.shape[0]
    scale = HEAD ** -0.5 * LOG2E
    lane = lax.broadcasted_iota(jnp.int32, (tm, HEAD), 1)
    row = lax.broadcasted_iota(jnp.int32, (HEAD, tm), 0)
    ones_rows = jnp.where(row < 3, 1.0, 0.0).astype(BF16)
    sum_row = jnp.where(lax.broadcasted_iota(jnp.int32, (VT_ROWS - HEAD, tm), 0) == 0, 1.0, 0.0)
    p = p_ref[...]
    for h in range(heads):
        sl = slice(h * HEAD, (h + 1) * HEAD)
        qa_ref[h, 0:HEAD, :] = (q_ref[:, sl].astype(F32) * scale).T.astype(BF16)
        qa_ref[h, HEAD:2 * HEAD, :] = ones_rows
        vt_ref[h, 0] = jnp.concatenate([v_ref[:, sl].astype(F32).T, sum_row], axis=0).astype(BF16)
        f_col = p[:, h:h + 1]
        e = f_col[0:1, :] - f_col
        e_hi = e.astype(BF16).astype(F32)
        e_mid = (e - e_hi).astype(BF16).astype(F32)
        e_lo = e - e_hi - e_mid
        cols = jnp.where(lane == 0, e_hi, jnp.where(lane == 1, e_mid, jnp.where(lane == 2, e_lo, 0.0)))
        ka_ref[h, 0, :, 0:HEAD] = k_ref[:, sl]
        ka_ref[h, 0, :, HEAD:2 * HEAD] = cols.astype(BF16)


def _fox_prep(proj, p, heads, blk):
    s = proj.shape[0]
    w = heads * HEAD
    return pl.pallas_call(
        functools.partial(_fox_prep_kernel, heads=heads),
        grid=(s // blk,),
        in_specs=[pl.BlockSpec((blk, w), lambda i: (i, 0)),
                  pl.BlockSpec((blk, w), lambda i: (i, 1)),
                  pl.BlockSpec((blk, w), lambda i: (i, 2)),
                  pl.BlockSpec((blk, LANES), lambda i: (i, 0))],
        out_specs=[pl.BlockSpec((heads, 2 * HEAD, blk), lambda i: (0, 0, i)),
                   pl.BlockSpec((heads, 1, blk, 2 * HEAD), lambda i: (0, i, 0, 0)),
                   pl.BlockSpec((heads, 1, VT_ROWS, blk), lambda i: (0, i, 0, 0))],
        out_shape=[jax.ShapeDtypeStruct((heads, 2 * HEAD, s), BF16),
                   jax.ShapeDtypeStruct((heads, s // blk, blk, 2 * HEAD), BF16),
                   jax.ShapeDtypeStruct((heads, s // blk, VT_ROWS, blk), BF16)],
        compiler_params=_cparams("parallel"),
        name="fox_prep",
    )(proj, proj, proj, p)


def _fox_kernel(fs_ref, qa_ref, ka_ref, vt_ref, gain_ref, o_ref, m_ref, acc_ref, *, blk, nsub):
    h = pl.program_id(0)
    i = pl.program_id(1)
    m_ref[...] = jnp.full_like(m_ref, NEG)
    acc_ref[...] = jnp.zeros_like(acc_ref)

    def steps(j, chains):
        ka = ka_ref[0, j]
        vt = vt_ref[0, j]
        f_k = fs_ref[h, j]
        scores = [jnp.dot(ka, qa_ref[0, :, a * blk:(a + 1) * blk], preferred_element_type=F32)
                  for a, _ in chains]
        probs, alphas = [], []
        for (a, diagonal), s in zip(chains, scores):
            if diagonal:
                kr = lax.broadcasted_iota(jnp.int32, s.shape, 0)
                qc = lax.broadcasted_iota(jnp.int32, s.shape, 1)
                s = jnp.where(kr <= qc, s, NEG)
            c = fs_ref[h, i * nsub + a] - f_k
            m_old = m_ref[a]
            m_new = jnp.maximum(m_old, jnp.max(s, axis=0, keepdims=True) + c)
            probs.append(jnp.exp2(s - (m_new - c)).astype(BF16))
            alphas.append(jnp.exp2(m_old - m_new))
            m_ref[a] = m_new
        for (a, _), p, alpha in zip(chains, probs, alphas):
            acc_ref[a] = alpha * acc_ref[a] + jnp.dot(vt, p, preferred_element_type=F32)

    def body(jj, carry):
        for t in range(nsub):
            steps(jj * nsub + t, [(a, False) for a in range(nsub)])
        return carry

    lax.fori_loop(0, i, body, 0)
    for t in range(nsub):
        steps(i * nsub + t, [(a, a == t) for a in range(t, nsub)])
    for a in range(nsub):
        acc = acc_ref[a]
        out = acc[0:HEAD] / acc[HEAD:HEAD + 1]
        out = out * lax.rsqrt(jnp.mean(out * out, axis=0, keepdims=True) + EPS)
        o_ref[a * blk:(a + 1) * blk, :] = (out.T * gain_ref[...]).astype(o_ref.dtype)


def _fox_attention(qa, ka, vt, fs, gain, blk):
    heads, _, s = qa.shape
    nk = s // blk
    nsub = NSUB if nk % NSUB == 0 else 1
    tq = nsub * blk
    vt_rows = vt.shape[2]
    grid_spec = pltpu.PrefetchScalarGridSpec(
        num_scalar_prefetch=1,
        grid=(heads, s // tq),
        in_specs=[pl.BlockSpec((1, 2 * HEAD, tq), lambda h, i, fs: (h, 0, i)),
                  pl.BlockSpec((1, nk, blk, 2 * HEAD), lambda h, i, fs: (h, 0, 0, 0)),
                  pl.BlockSpec((1, nk, vt_rows, blk), lambda h, i, fs: (h, 0, 0, 0)),
                  pl.BlockSpec((1, HEAD), lambda h, i, fs: (0, 0))],
        out_specs=pl.BlockSpec((tq, HEAD), lambda h, i, fs: (i, h)),
        scratch_shapes=[pltpu.VMEM((nsub, 1, blk), F32), pltpu.VMEM((nsub, vt_rows, blk), F32)],
    )
    return pl.pallas_call(
        functools.partial(_fox_kernel, blk=blk, nsub=nsub),
        grid_spec=grid_spec,
        out_shape=jax.ShapeDtypeStruct((s, heads * HEAD), BF16),
        compiler_params=_cparams("parallel", "arbitrary"),
        name="fox_attention",
    )(fs, qa, ka, vt, gain.reshape(1, HEAD))


def _gdn_prep_kernel(x_ref, prev_ref, w_ref, o_ref, buf_ref, *, heads):
    tm = x_ref.shape[0]
    halo = prev_ref.shape[0]
    prev = prev_ref[...].astype(F32)
    buf_ref[0:halo, :] = jnp.where(pl.program_id(0) == 0, 0.0, prev)
    buf_ref[halo:halo + tm, :] = x_ref[...].astype(F32)
    y = jnp.zeros(x_ref.shape, F32)
    for s in range(CONV):
        y = y + buf_ref[halo - s:halo - s + tm, :] * w_ref[CONV - 1 - s:CONV - s, :]
    y = y * _sigmoid(y)
    w = heads * HEAD
    for h in range(2 * heads):
        sl = slice(h * HEAD, (h + 1) * HEAD)
        v = y[:, sl]
        n = v * lax.rsqrt(jnp.sum(v * v, axis=-1, keepdims=True) + EPS)
        if h < heads:
            n = n * HEAD ** -0.5
        o_ref[:, sl] = n.astype(o_ref.dtype)
    o_ref[:, 2 * w:3 * w] = y[:, 2 * w:3 * w].astype(o_ref.dtype)


def _gdn_prep(proj, conv_w, heads):
    s = proj.shape[0]
    w3 = 3 * heads * HEAD
    tm = _pick(s, 256, 16)
    halo = 16
    per = tm // halo
    return pl.pallas_call(
        functools.partial(_gdn_prep_kernel, heads=heads),
        grid=(s // tm,),
        in_specs=[pl.BlockSpec((tm, w3), lambda i: (i, 1)),
                  pl.BlockSpec((halo, w3), lambda i: (jnp.maximum(i * per - 1, 0), 1)),
                  pl.BlockSpec((CONV, w3), lambda i: (0, 0))],
        out_specs=pl.BlockSpec((tm, w3), lambda i: (i, 0)),
        out_shape=jax.ShapeDtypeStruct((s, w3), BF16),
        scratch_shapes=[pltpu.VMEM((tm + halo, w3), F32)],
        compiler_params=_cparams("parallel"),
        name="gdn_prep",
    )(proj, proj, conv_w)


def _bdot(a, b):
    return jnp.dot(a.astype(BF16), b.astype(BF16), preferred_element_type=F32)


def _bdot_nt(a, b):
    return lax.dot_general(a.astype(BF16), b.astype(BF16), (((1,), (1,)), ((), ())),
                           preferred_element_type=F32)


def _gdn_kernel(q_ref, k_ref, v_ref, z_ref, pt_ref, gain_ref, o_ref, state_ref, *, heads, group):
    hg = pl.program_id(0)

    @pl.when(pl.program_id(1) == 0)
    def _():
        state_ref[...] = jnp.zeros_like(state_ref)

    tb = q_ref.shape[0]
    c = GDN_CHUNK
    ri = lax.broadcasted_iota(jnp.int32, (tb, tb), 0)
    ci = lax.broadcasted_iota(jnp.int32, (tb, tb), 1)
    rc_xor = jnp.bitwise_xor(ri, ci)
    same_chunk = (rc_xor >> (c.bit_length() - 1)) == 0
    causal = same_chunk & (ri >= ci)
    eye = jnp.where(ri == ci, 1.0, 0.0).astype(F32)

    def col(r):
        return jnp.broadcast_to(r, (HEAD, tb)).T

    hs = range(group)
    sls = [slice(gi * HEAD, (gi + 1) * HEAD) for gi in hs]
    g_row = [pt_ref[pl.ds(heads + hg * group + gi, 1), :] for gi in hs]
    b_row = [pt_ref[pl.ds(2 * heads + hg * group + gi, 1), :] for gi in hs]
    gl_row = [pt_ref[pl.ds(3 * heads + hg * group + gi, 1), :] for gi in hs]
    g_col = [col(r) for r in g_row]
    b_col = [col(r) for r in b_row]
    e_g = [jnp.exp(x) for x in g_col]
    e_tail = [jnp.exp(col(gl_row[gi]) - g_col[gi]) for gi in hs]
    q = [q_ref[:, sl].astype(F32) for sl in sls]
    k = [k_ref[:, sl].astype(F32) for sl in sls]
    v = [v_ref[:, sl].astype(F32) for sl in sls]
    kb = [k[gi] * b_col[gi] for gi in hs]
    kk = [_bdot_nt(kb[gi], k[gi]) for gi in hs]
    qk = [_bdot_nt(q[gi], k[gi]) for gi in hs]
    decay = [jnp.exp(jnp.where(causal, g_col[gi][:, 0:1] - g_row[gi], NEG)) for gi in hs]
    lower = [jnp.where(ri > ci, kk[gi] * decay[gi], 0.0) for gi in hs]
    attn = [qk[gi] * decay[gi] for gi in hs]
    inv = [eye - jnp.where(rc_xor == 1, lower[gi], 0.0) for gi in hs]
    b = 2
    while b < c:
        join = (rc_xor >> (b.bit_length() - 1)) == 1
        t = [_bdot(jnp.where(join, lower[gi], 0.0), inv[gi]) for gi in hs]
        inv = [inv[gi] - _bdot(inv[gi], t[gi]) for gi in hs]
        b *= 2
    uw = [_bdot(inv[gi], jnp.concatenate([v[gi] * b_col[gi], kb[gi] * e_g[gi]], axis=1)) for gi in hs]
    qg = [q[gi] * e_g[gi] for gi in hs]
    kt_t = [(k[gi] * e_tail[gi]).T for gi in hs]
    state = [state_ref[gi] for gi in hs]
    outs = [[] for _ in hs]
    for ch in range(tb // c):
        r = slice(ch * c, (ch + 1) * c)
        x = [_bdot(jnp.concatenate([uw[gi][r, HEAD:2 * HEAD], qg[gi][r]], axis=0), state[gi]) for gi in hs]
        v_new = [uw[gi][r, 0:HEAD] - x[gi][0:c] for gi in hs]
        for gi in hs:
            outs[gi].append(x[gi][c:2 * c] + _bdot(attn[gi][r, r], v_new[gi]))
        state = [state[gi] * jnp.exp(gl_row[gi][:, ch * c:ch * c + 1]) + _bdot(kt_t[gi][:, r], v_new[gi]) for gi in hs]
    for gi in hs:
        state_ref[gi] = state[gi]
        o = jnp.concatenate(outs[gi], axis=0)
        z = z_ref[:, sls[gi]].astype(F32)
        o = _rms(o) * gain_ref[...] * (z * _sigmoid(z))
        o_ref[:, sls[gi]] = o.astype(o_ref.dtype)


def _gdn(qkv, proj, pt, gain, heads, group=GDN_GROUP):
    s = qkv.shape[0]
    tb = _pick(s, 256, GDN_CHUNK)
    group = min(group, heads)
    gw = group * HEAD
    per = heads // group
    return pl.pallas_call(
        functools.partial(_gdn_kernel, heads=heads, group=group),
        grid=(per, s // tb),
        in_specs=[pl.BlockSpec((tb, gw), lambda g, i: (i, g)),
                  pl.BlockSpec((tb, gw), lambda g, i: (i, per + g)),
                  pl.BlockSpec((tb, gw), lambda g, i: (i, 2 * per + g)),
                  pl.BlockSpec((tb, gw), lambda g, i: (i, 6 * per + g)),
                  pl.BlockSpec((LANES, tb), lambda g, i: (0, i)),
                  pl.BlockSpec((1, HEAD), lambda g, i: (0, 0))],
        out_specs=pl.BlockSpec((tb, gw), lambda g, i: (i, g)),
        out_shape=jax.ShapeDtypeStruct((s, heads * HEAD), BF16),
        scratch_shapes=[pltpu.VMEM((group, HEAD, HEAD), F32)],
        compiler_params=_cparams("parallel", "arbitrary"),
        name="gdn",
    )(qkv, qkv, qkv, proj, pt, gain.reshape(1, HEAD))


def _ffn_kernel(h_ref, wg_ref, wu_ref, wd_ref, o_ref, acc_ref):
    f = pl.program_id(1)

    @pl.when(f == 0)
    def _():
        acc_ref[...] = jnp.zeros_like(acc_ref)

    h = h_ref[...]
    a = jnp.dot(h, wg_ref[...], preferred_element_type=F32)
    b = jnp.dot(h, wu_ref[...], preferred_element_type=F32)
    acc_ref[...] += jnp.dot((a * _sigmoid(a) * b).astype(BF16), wd_ref[...], preferred_element_type=F32)

    @pl.when(f == pl.num_programs(1) - 1)
    def _():
        o_ref[...] = acc_ref[...].astype(o_ref.dtype)


def _ffn_dense(h, wg, wu, wd, layer, out_dtype):
    s, d = h.shape
    f = wg.shape[2]
    tm = _pick(s, 512)
    tf = _pick(f, 256)
    return pl.pallas_call(
        _ffn_kernel,
        grid=(s // tm, f // tf),
        in_specs=[pl.BlockSpec((tm, d), lambda i, j: (i, 0)),
                  pl.BlockSpec((None, d, tf), lambda i, j: (layer, 0, j)),
                  pl.BlockSpec((None, d, tf), lambda i, j: (layer, 0, j)),
                  pl.BlockSpec((None, tf, d), lambda i, j: (layer, j, 0))],
        out_specs=pl.BlockSpec((tm, d), lambda i, j: (i, 0)),
        out_shape=jax.ShapeDtypeStruct((s, d), out_dtype),
        scratch_shapes=[pltpu.VMEM((tm, d), F32)],
        compiler_params=_cparams("parallel", "arbitrary"),
        name="ffn_dense",
    )(h, wg, wu, wd)


def _router_kernel(h_ref, w_ref, o_ref, *, experts):
    lo, hi = _unpack_pairs(h_ref[...])
    half = lo.shape[1]
    logits = (jnp.dot(lo.astype(BF16), w_ref[0:half, :], preferred_element_type=F32)
              + jnp.dot(hi.astype(BF16), w_ref[half:2 * half, :], preferred_element_type=F32))
    lane = lax.broadcasted_iota(jnp.int32, logits.shape, 1).astype(F32)
    logits = jnp.where(lane < experts, logits, -jnp.inf)
    m1 = jnp.max(logits, axis=-1, keepdims=True)
    i1 = jnp.min(jnp.where(logits == m1, lane, LANES), axis=-1, keepdims=True)
    rest = jnp.where(lane == i1, -jnp.inf, logits)
    m2 = jnp.max(rest, axis=-1, keepdims=True)
    i2 = jnp.min(jnp.where(rest == m2, lane, LANES), axis=-1, keepdims=True)
    e2 = jnp.exp(m2 - m1)
    w1 = 1.0 / (1.0 + e2)
    w2 = e2 / (1.0 + e2)
    hit = jnp.where((lane == i1) | (lane == i2), 1.0, 0.0)
    info = jnp.where(lane == experts, i1,
                     jnp.where(lane == experts + 1, i2,
                               jnp.where(lane == experts + 2, w1,
                                         jnp.where(lane == experts + 3, w2, hit))))
    o_ref[...] = info


def _router(h, w_router):
    s, dh = h.shape
    d = 2 * dh
    e = w_router.shape[1]
    assert e + 4 <= LANES and w_router.shape[0] == d
    wr = jnp.zeros((d, LANES), BF16).at[:, :e].set(w_router.astype(BF16))
    tm = _pick(s, 256, 8)
    return pl.pallas_call(
        functools.partial(_router_kernel, experts=e),
        grid=(s // tm,),
        in_specs=[pl.BlockSpec((tm, dh), lambda i: (i, 0)),
                  pl.BlockSpec((d, LANES), lambda i: (0, 0))],
        out_specs=pl.BlockSpec((tm, LANES), lambda i: (i, 0)),
        out_shape=jax.ShapeDtypeStruct((s, LANES), F32),
        compiler_params=_cparams("parallel"),
        name="router",
    )(h, wr)


def _count_kernel(r_ref, o_ref, carry_ref, *, experts):
    @pl.when(pl.program_id(0) == 0)
    def _():
        carry_ref[...] = jnp.zeros_like(carry_ref)

    rows = r_ref.shape[0]
    lane = lax.broadcasted_iota(jnp.int32, (1, LANES), 1)
    ri = lax.broadcasted_iota(jnp.int32, (rows, rows), 0)
    ci = lax.broadcasted_iota(jnp.int32, (rows, rows), 1)
    tri = jnp.where(ri >= ci, 1.0, 0.0).astype(BF16)
    hit = jnp.where(lane < experts, r_ref[...], 0.0)
    run = jnp.dot(tri, hit.astype(BF16), preferred_element_type=F32) + carry_ref[...]
    carry_ref[...] = run[rows - 1:rows, :]
    o_ref[...] = run


def _running_counts(info, experts):
    s = info.shape[0]
    rows = _pick(s, 256, 8)
    return pl.pallas_call(
        functools.partial(_count_kernel, experts=experts),
        grid=(s // rows,),
        in_specs=[pl.BlockSpec((rows, LANES), lambda i: (i, 0))],
        out_specs=pl.BlockSpec((rows, LANES), lambda i: (i, 0)),
        out_shape=jax.ShapeDtypeStruct((s, LANES), F32),
        scratch_shapes=[pltpu.VMEM((1, LANES), F32)],
        compiler_params=_cparams("arbitrary"),
        name="running_counts",
    )(info)


def _row_copy(src, dst, sem, src_row, dst_row):
    return pltpu.make_async_copy(src.at[pl.ds(src_row, 1)], dst.at[pl.ds(dst_row, 1)], sem)


def _dispatch_kernel(pos_ref, h_ref, init_ref, o_ref, sem):
    del init_ref
    tm = h_ref.shape[0]

    def issue(t, carry):
        for slot in range(TOP_K):
            _row_copy(h_ref, o_ref, sem, t, pos_ref[TOP_K * t + slot]).start()
        return carry

    lax.fori_loop(0, tm, issue, 0)

    def drain(t, carry):
        for slot in range(TOP_K):
            _row_copy(h_ref, o_ref, sem, t, pos_ref[TOP_K * t + slot]).wait()
        return carry

    lax.fori_loop(0, tm, drain, 0)


def _dispatch(h, pos_flat, rows_padded):
    s, d = h.shape
    tm = _pick(s, 256, 8)
    init = jnp.zeros((rows_padded, d), h.dtype)
    return pl.pallas_call(
        _dispatch_kernel,
        grid=(s // tm,),
        in_specs=[pl.BlockSpec((TOP_K * tm,), lambda i: (i,), memory_space=pltpu.SMEM),
                  pl.BlockSpec((tm, d), lambda i: (i, 0)),
                  pl.BlockSpec(memory_space=pl.ANY)],
        out_specs=pl.BlockSpec(memory_space=pl.ANY),
        out_shape=jax.ShapeDtypeStruct((rows_padded, d), h.dtype),
        scratch_shapes=[pltpu.SemaphoreType.DMA(())],
        input_output_aliases={2: 0},
        compiler_params=_cparams("arbitrary"),
        name="moe_dispatch",
    )(pos_flat, h, init)


def _moe_kernel(te_ref, used_ref, h_ref, wg_ref, wu_ref, wd_ref, o_ref, hb_ref, acc_ref):
    del te_ref
    i = pl.program_id(0)
    f = pl.program_id(1)
    active = i < used_ref[0]
    half = h_ref.shape[1]

    @pl.when(f == 0)
    def _():
        lo, hi = _unpack_pairs(h_ref[...])
        hb_ref[:, 0:half] = lo.astype(BF16)
        hb_ref[:, half:2 * half] = hi.astype(BF16)
        acc_ref[...] = jnp.zeros_like(acc_ref)

    @pl.when(active)
    def _():
        h = hb_ref[...]
        a = jnp.dot(h, wg_ref[...], preferred_element_type=F32)
        b = jnp.dot(h, wu_ref[...], preferred_element_type=F32)
        acc_ref[...] += jnp.dot((a * _sigmoid(a) * b).astype(BF16), wd_ref[...], preferred_element_type=F32)

    @pl.when(f == pl.num_programs(1) - 1)
    def _():
        o_ref[...] = _pack_pairs(acc_ref[...])


def _moe_grouped(hs, tile_expert, tiles_used, wg, wu, wd, layer, tm):
    rows, dh = hs.shape
    d = 2 * dh
    f = wg.shape[3]
    tf = _pick(f, 256)
    grid_spec = pltpu.PrefetchScalarGridSpec(
        num_scalar_prefetch=2,
        grid=(rows // tm, f // tf),
        in_specs=[pl.BlockSpec((tm, dh), lambda i, j, te, nu: (i, 0)),
                  pl.BlockSpec((None, None, d, tf), lambda i, j, te, nu: (layer, te[i], 0, j)),
                  pl.BlockSpec((None, None, d, tf), lambda i, j, te, nu: (layer, te[i], 0, j)),
                  pl.BlockSpec((None, None, tf, d), lambda i, j, te, nu: (layer, te[i], j, 0))],
        out_specs=pl.BlockSpec((tm, dh), lambda i, j, te, nu: (i, 0)),
        scratch_shapes=[pltpu.VMEM((tm, d), BF16), pltpu.VMEM((tm, d), F32)],
    )
    return pl.pallas_call(
        _moe_kernel,
        grid_spec=grid_spec,
        out_shape=jax.ShapeDtypeStruct((rows, dh), jnp.uint32),
        compiler_params=_cparams("arbitrary", "arbitrary"),
        name="moe_grouped",
    )(tile_expert, tiles_used, hs, wg, wu, wd)


def _combine_kernel(pos_ref, x_ref, info_ref, gpost_ref, mod_ref, gpre_ref, modn_ref, ys_ref,
                    xo_ref, *rest, experts, gate_row, shift_row, scale_row):
    *h_refs, buf_ref, sem = rest
    tm = x_ref.shape[0]

    def issue(t, carry):
        for slot in range(TOP_K):
            _row_copy(ys_ref, buf_ref.at[slot], sem, pos_ref[TOP_K * t + slot], t).start()
        return carry

    lax.fori_loop(0, tm, issue, 0)

    def drain(t, carry):
        for slot in range(TOP_K):
            _row_copy(ys_ref, buf_ref.at[slot], sem, pos_ref[TOP_K * t + slot], t).wait()
        return carry

    lax.fori_loop(0, tm, drain, 0)
    info = info_ref[...]
    w1 = info[:, experts + 2:experts + 3]
    w2 = info[:, experts + 3:experts + 4]
    lo1, hi1 = _unpack_pairs(buf_ref[0])
    lo2, hi2 = _unpack_pairs(buf_ref[1])
    y = jnp.concatenate([w1 * lo1 + w2 * lo2, w1 * hi1 + w2 * hi2], axis=1)
    x = _resid_update(x_ref[...], y, gpost_ref[...], mod_ref[...], gate_row)
    xo_ref[...] = x
    if h_refs:
        h_ref, = h_refs
        _store_h(h_ref, _next_h(x, gpre_ref[...], modn_ref[...], shift_row, scale_row))


def _moe_combine_resid(x, ys, info, pos_flat, experts, gpost, mod, gate_row,
                       gpre=None, mod_next=None, shift_row=0, scale_row=0, h_dtype=None):
    s, d = x.shape
    tm = _pick(s, 256, 8)
    want_h = h_dtype is not None
    if not want_h:
        gpre, mod_next = gpost, mod
    row = pl.BlockSpec((tm, d), lambda i: (i, 0))
    vec = pl.BlockSpec((1, d), lambda i: (0, 0))
    tab = pl.BlockSpec((N_MOD, d), lambda i: (0, 0))
    out_shape = [jax.ShapeDtypeStruct((s, d), F32)]
    out_specs = [row]
    if want_h:
        out_shape.append(jax.ShapeDtypeStruct((s, d), h_dtype))
        out_specs.append(row)
    res = pl.pallas_call(
        functools.partial(_combine_kernel, experts=experts, gate_row=gate_row,
                          shift_row=shift_row, scale_row=scale_row),
        grid=(s // tm,),
        in_specs=[pl.BlockSpec((TOP_K * tm,), lambda i: (i,), memory_space=pltpu.SMEM),
                  row,
                  pl.BlockSpec((tm, LANES), lambda i: (i, 0)),
                  vec, tab, vec, tab,
                  pl.BlockSpec(memory_space=pl.ANY)],
        out_specs=out_specs,
        out_shape=out_shape,
        scratch_shapes=[pltpu.VMEM((TOP_K, tm, d // 2), jnp.uint32), pltpu.SemaphoreType.DMA(())],
        compiler_params=_cparams("arbitrary"),
        name="moe_combine",
    )(pos_flat, x, info, gpost.reshape(1, d), mod, gpre.reshape(1, d), mod_next, ys)
    return (res[0], res[1]) if want_h else (res[0], None)


def _dispatch_table(info, counts, experts, tm):
    s = info.shape[0]
    idx = info[:, experts:experts + TOP_K].astype(jnp.int32)
    rank = jnp.take_along_axis(counts[:, :experts], idx, axis=1).astype(jnp.int32) - 1
    total = counts[s - 1, :experts].astype(jnp.int32)
    padded = ((total + tm - 1) // tm) * tm
    ends = jnp.cumsum(padded)
    starts = ends - padded
    pos = (starts[idx] + rank).reshape(-1)
    n_tiles = (TOP_K * s) // tm + experts
    tile_start = jnp.arange(n_tiles, dtype=jnp.int32) * tm
    tile_expert = jnp.minimum(jnp.sum(tile_start[:, None] >= ends[None, :], axis=1), experts - 1)
    tiles_used = (ends[experts - 1] // tm).reshape(1)
    return pos.astype(jnp.int32), tile_expert.astype(jnp.int32), tiles_used.astype(jnp.int32), n_tiles * tm


def kernel(x, c, w_c, b_c, mod_table, pre_mix_norm, post_mix_norm, pre_ffn_norm, post_ffn_norm, w_in, w_out, conv_w, b_f, a_log, dt_bias, fox_norm, gdn_norm, w_gate_dense, w_up_dense, w_down_dense, w_router, w_gate_moe, w_up_moe, w_down_moe):
    batch, seq, d = x.shape
    assert batch == 1
    depth = mod_table.shape[0]
    half = d // 2
    heads = half // HEAD
    experts = w_router.shape[2]
    blk = _pick(seq, 512)

    o_ff = 3 * half
    o_g = o_ff + heads
    o_ga = o_g + 3 * half
    o_gz = o_ga + 2 * heads
    w_main, w_small = _regroup_w_in(jnp.swapaxes(w_in, 1, 2), ((0, o_ff), (o_g, o_ga), (o_gz, w_in.shape[2])),
                                    ((o_ff, o_g), (o_ga, o_gz)))
    w_out_b = w_out.astype(BF16)
    w_dense = [w.astype(BF16) for w in (w_gate_dense, w_up_dense, w_down_dense)]
    w_moe = [w.astype(BF16) for w in (w_gate_moe, w_up_moe, w_down_moe)]

    mods = _mod_table(c, w_c, b_c, mod_table)
    xs = x.reshape(seq, d)
    moe_tm = _pick(seq, 512)

    h = _prenorm(xs, pre_mix_norm[0], mods[0], 0, 1, BF16)
    for l in range(depth):
        is_moe = l % 2 == 1
        j = l // 2
        proj = _matmul_nt(h, w_main, l, BF16, name="in_proj")
        small = _matmul_nt(h, w_small, l, F32, name="in_proj_small")
        p, pt = _gate_scalars(small, b_f[l], a_log[l], dt_bias[l])
        qa, ka, vt = _fox_prep(proj, p, heads, blk)
        fs = p[::blk, :heads].T
        o_fox = _fox_attention(qa, ka, vt, fs, fox_norm[l], blk)
        qkv = _gdn_prep(proj, conv_w[l], heads)
        o_gdn = _gdn(qkv, proj, pt, gdn_norm[l], heads)
        y = _out_proj(o_fox, o_gdn, w_out_b, l, BF16)
        xs, h = _resid(xs, y, post_mix_norm[l], mods[l], 2, pre_ffn_norm[l], mods[l], 3, 4,
                       jnp.uint32 if is_moe else BF16)
        last = l == depth - 1
        nxt = dict(gpre=None if last else pre_mix_norm[l + 1], mod_next=None if last else mods[l + 1],
                   shift_row=0, scale_row=1, h_dtype=None if last else BF16)
        if not is_moe:
            y = _ffn_dense(h, *w_dense, j, BF16)
            xs, h = _resid(xs, y, post_ffn_norm[l], mods[l], 5, **nxt)
        else:
            info = _router(h, w_router[j])
            counts = _running_counts(info, experts)
            pos, tile_expert, tiles_used, rows_padded = _dispatch_table(info, counts, experts, moe_tm)
            hs = _dispatch(h, pos, rows_padded)
            ys = _moe_grouped(hs, tile_expert, tiles_used, *w_moe, j, moe_tm)
            xs, h = _moe_combine_resid(xs, ys, info, pos, experts, post_ffn_norm[l], mods[l], 5, **nxt)
    return xs.reshape(batch, seq, d)
```

```python
import functools

import jax
import jax.numpy as jnp
from jax import lax
from jax.experimental import pallas as pl
from jax.experimental.pallas import tpu as pltpu

F32 = jnp.float32
BF16 = jnp.bfloat16

HEAD = 128
LANES = 128
N_MOD = 6
EPS = 1e-6
CONV = 4
TOP_K = 2
LOG2E = 1.4426950408889634
NEG = -1e30
GDN_CHUNK = 128
VT_ROWS = HEAD + 16
GDN_GROUP = 8
NSUB = 4
VMEM_LIMIT_BYTES = 56 * 1024 * 1024


def _pick(n, pref, mult=LANES):
    if n <= pref:
        return n
    t = (pref // mult) * mult
    while t >= mult:
        if n % t == 0:
            return t
        t -= mult
    return n


def _cparams(*sem):
    return pltpu.CompilerParams(dimension_semantics=sem, vmem_limit_bytes=VMEM_LIMIT_BYTES)


def _rms(x):
    return x * lax.rsqrt(jnp.mean(x * x, axis=-1, keepdims=True) + EPS)


def _sigmoid(x):
    return 1.0 / (1.0 + jnp.exp(-x))


def _mod_kernel(c_ref, w_ref, b_ref, tab_ref, o_ref):
    c = c_ref[...]
    sc = c * _sigmoid(c)
    lhs = jnp.broadcast_to(sc, (8, sc.shape[1])).astype(BF16)
    base = jnp.dot(lhs, w_ref[...].astype(BF16), preferred_element_type=F32)[0:1]
    o_ref[...] = base + b_ref[...] + tab_ref[...]


def _mod_table(c, w_c, b_c, mod_table):
    depth = mod_table.shape[0]
    d = c.shape[1]
    n = w_c.shape[1]
    tn = _pick(n, 512)
    out = pl.pallas_call(
        _mod_kernel,
        grid=(n // tn,),
        in_specs=[pl.BlockSpec((1, d), lambda j: (0, 0)),
                  pl.BlockSpec((d, tn), lambda j: (0, j)),
                  pl.BlockSpec((1, tn), lambda j: (0, j)),
                  pl.BlockSpec((depth, tn), lambda j: (0, j))],
        out_specs=pl.BlockSpec((depth, tn), lambda j: (0, j)),
        out_shape=jax.ShapeDtypeStruct((depth, n), F32),
        compiler_params=_cparams("arbitrary"),
        name="mod_table",
    )(c, w_c, b_c.reshape(1, n), mod_table.reshape(depth, n))
    return out.reshape(depth, N_MOD, d)


def _prenorm_kernel(x_ref, g_ref, mod_ref, h_ref, *, shift_row, scale_row):
    h = _rms(x_ref[...]) * g_ref[...]
    h = h * (1.0 + mod_ref[scale_row:scale_row + 1, :]) + mod_ref[shift_row:shift_row + 1, :]
    h_ref[...] = h.astype(h_ref.dtype)


def _prenorm(x, gain, mod, shift_row, scale_row, out_dtype):
    s, d = x.shape
    tm = _pick(s, 256, 8)
    return pl.pallas_call(
        functools.partial(_prenorm_kernel, shift_row=shift_row, scale_row=scale_row),
        grid=(s // tm,),
        in_specs=[pl.BlockSpec((tm, d), lambda i: (i, 0)),
                  pl.BlockSpec((1, d), lambda i: (0, 0)),
                  pl.BlockSpec((N_MOD, d), lambda i: (0, 0))],
        out_specs=pl.BlockSpec((tm, d), lambda i: (i, 0)),
        out_shape=jax.ShapeDtypeStruct((s, d), out_dtype),
        compiler_params=_cparams("parallel"),
        name="prenorm",
    )(x, gain.reshape(1, d), mod)


def _resid_update(x, y, gpost, mod, gate_row):
    yn = _rms(y) * gpost
    return x + mod[gate_row:gate_row + 1, :] * yn


def _next_h(x, gpre, mod_next, shift_row, scale_row):
    h = _rms(x) * gpre
    return h * (1.0 + mod_next[scale_row:scale_row + 1, :]) + mod_next[shift_row:shift_row + 1, :]


def _pack_pairs(x):
    half = x.shape[1] // 2
    lo = lax.bitcast_convert_type(x[:, :half].astype(BF16).astype(F32), jnp.uint32) >> 16
    hi = lax.bitcast_convert_type(x[:, half:].astype(BF16).astype(F32), jnp.uint32) & jnp.uint32(0xFFFF0000)
    return lo | hi


def _unpack_pairs(u):
    lo = lax.bitcast_convert_type(u << 16, F32)
    hi = lax.bitcast_convert_type(u & jnp.uint32(0xFFFF0000), F32)
    return lo, hi


def _store_h(h_ref, h):
    h_ref[...] = _pack_pairs(h) if h_ref.dtype == jnp.uint32 else h.astype(h_ref.dtype)


def _resid_kernel(x_ref, y_ref, gpost_ref, mod_ref, gpre_ref, modn_ref, xo_ref, *h_refs,
                  gate_row, shift_row, scale_row):
    x = _resid_update(x_ref[...], y_ref[...].astype(F32), gpost_ref[...], mod_ref[...], gate_row)
    xo_ref[...] = x
    if h_refs:
        h_ref, = h_refs
        _store_h(h_ref, _next_h(x, gpre_ref[...], modn_ref[...], shift_row, scale_row))


def _resid(x, y, gpost, mod, gate_row, gpre=None, mod_next=None, shift_row=0, scale_row=0, h_dtype=None):
    s, d = x.shape
    tm = _pick(s, 256, 8)
    want_h = h_dtype is not None
    if not want_h:
        gpre, mod_next = gpost, mod
    row = pl.BlockSpec((tm, d), lambda i: (i, 0))
    vec = pl.BlockSpec((1, d), lambda i: (0, 0))
    tab = pl.BlockSpec((N_MOD, d), lambda i: (0, 0))
    out_shape = [jax.ShapeDtypeStruct((s, d), F32)]
    out_specs = [row]
    if want_h:
        dh = d // 2 if h_dtype == jnp.uint32 else d
        out_shape.append(jax.ShapeDtypeStruct((s, dh), h_dtype))
        out_specs.append(pl.BlockSpec((tm, dh), lambda i: (i, 0)))
    res = pl.pallas_call(
        functools.partial(_resid_kernel, gate_row=gate_row, shift_row=shift_row, scale_row=scale_row),
        grid=(s // tm,),
        in_specs=[row, row, vec, tab, vec, tab],
        out_specs=out_specs,
        out_shape=out_shape,
        compiler_params=_cparams("parallel"),
        name="resid",
    )(x, y, gpost.reshape(1, d), mod, gpre.reshape(1, d), mod_next)
    return (res[0], res[1]) if want_h else (res[0], None)


def _mm_kernel(a_ref, b_ref, o_ref):
    o_ref[...] = jnp.dot(a_ref[...], b_ref[...], preferred_element_type=F32).astype(o_ref.dtype)


def _matmul(a, b, layer, out_dtype, tm_pref=1024, tn_pref=1024, name="matmul"):
    m, k = a.shape
    n = b.shape[2]
    tm = _pick(m, tm_pref)
    tn = _pick(n, tn_pref)
    return pl.pallas_call(
        _mm_kernel,
        grid=(m // tm, n // tn),
        in_specs=[pl.BlockSpec((tm, k), lambda i, j: (i, 0)),
                  pl.BlockSpec((None, k, tn), lambda i, j: (layer, 0, j))],
        out_specs=pl.BlockSpec((tm, tn), lambda i, j: (i, j)),
        out_shape=jax.ShapeDtypeStruct((m, n), out_dtype),
        compiler_params=_cparams("parallel", "arbitrary"),
        name=name,
    )(a, b)


def _mm_nt_kernel(a_ref, bt_ref, o_ref):
    acc = lax.dot_general(a_ref[...], bt_ref[...], (((1,), (1,)), ((), ())), preferred_element_type=F32)
    o_ref[...] = acc.astype(o_ref.dtype)


def _matmul_nt(a, bt, layer, out_dtype, tm_pref=1024, tn_pref=1024, name="matmul_nt"):
    m, k = a.shape
    n = bt.shape[1]
    tm = _pick(m, tm_pref)
    tn = _pick(n, tn_pref)
    return pl.pallas_call(
        _mm_nt_kernel,
        grid=(m // tm, n // tn),
        in_specs=[pl.BlockSpec((tm, k), lambda i, j: (i, 0)),
                  pl.BlockSpec((None, tn, k), lambda i, j: (layer, j, 0))],
        out_specs=pl.BlockSpec((tm, tn), lambda i, j: (i, j)),
        out_shape=jax.ShapeDtypeStruct((m, n), out_dtype),
        compiler_params=_cparams("parallel", "arbitrary"),
        name=name,
    )(a, bt)


def _mm2_kernel(a1_ref, a2_ref, b1_ref, b2_ref, o_ref):
    acc = jnp.dot(a1_ref[...], b1_ref[...], preferred_element_type=F32)
    acc = acc + jnp.dot(a2_ref[...], b2_ref[...], preferred_element_type=F32)
    o_ref[...] = acc.astype(o_ref.dtype)


def _out_proj(a1, a2, w, layer, out_dtype):
    m, k1 = a1.shape
    k2 = a2.shape[1]
    assert k1 == k2 and w.shape[1] == k1 + k2
    n = w.shape[2]
    tm = _pick(m, 1024)
    tn = _pick(n, 1024)
    return pl.pallas_call(
        _mm2_kernel,
        grid=(m // tm, n // tn),
        in_specs=[pl.BlockSpec((tm, k1), lambda i, j: (i, 0)),
                  pl.BlockSpec((tm, k2), lambda i, j: (i, 0)),
                  pl.BlockSpec((None, k1, tn), lambda i, j: (layer, 0, j)),
                  pl.BlockSpec((None, k2, tn), lambda i, j: (layer, 1, j))],
        out_specs=pl.BlockSpec((tm, tn), lambda i, j: (i, j)),
        out_shape=jax.ShapeDtypeStruct((m, n), out_dtype),
        compiler_params=_cparams("parallel", "arbitrary"),
        name="out_proj",
    )(a1, a2, w, w)


def _regroup_kernel(w_ref, main_ref, small_ref, buf_ref, sbuf_ref, sem, ssem, *, wide, narrow, rows, blocks):
    l = pl.program_id(0)
    i = pl.program_id(1)
    step = l * blocks + i

    def src_copy(t, slot):
        tl = t // blocks
        ti = t - tl * blocks
        row = ti * rows
        at = 0
        for lo, hi in wide:
            row = row + jnp.where(ti * rows >= at, lo - at, 0) - jnp.where(ti * rows >= at + hi - lo, lo - at, 0)
            at += hi - lo
        return pltpu.make_async_copy(w_ref.at[tl, pl.ds(pl.multiple_of(row, 16), rows)], buf_ref.at[slot], sem.at[slot])

    @pl.when(step == 0)
    def _():
        src_copy(step, 0).start()

    @pl.when(step + 1 < pl.num_programs(0) * blocks)
    def _():
        src_copy(step + 1, (step + 1) % 2).start()

    @pl.when(i == 0)
    def _():
        sbuf_ref[...] = jnp.zeros_like(sbuf_ref)
        at = 0
        copies = []
        for lo, hi in narrow:
            copies.append(pltpu.make_async_copy(w_ref.at[l, lo:hi], sbuf_ref.at[at:at + hi - lo], ssem))
            at += hi - lo
        for cp in copies:
            cp.start()
        for cp in copies:
            cp.wait()
        small_ref[...] = sbuf_ref[...].astype(small_ref.dtype)

    src_copy(step, step % 2).wait()
    main_ref[...] = buf_ref[step % 2].astype(main_ref.dtype)


def _regroup_w_in(w_t, wide, narrow):
    depth, n, k = w_t.shape
    sizes = [hi - lo for lo, hi in wide]
    n_main = sum(sizes)
    rows = 512
    while any(sz % rows for sz in sizes):
        rows //= 2
    assert rows >= 16 and all(lo % 16 == 0 for lo, _ in wide + narrow)
    assert sum(hi - lo for lo, hi in narrow) <= LANES and all((hi - lo) % 16 == 0 for lo, hi in narrow)
    blocks = n_main // rows
    return pl.pallas_call(
        functools.partial(_regroup_kernel, wide=wide, narrow=narrow, rows=rows, blocks=blocks),
        grid=(depth, blocks),
        in_specs=[pl.BlockSpec(memory_space=pl.ANY)],
        out_specs=[pl.BlockSpec((None, rows, k), lambda l, i: (l, i, 0)),
                   pl.BlockSpec((None, LANES, k), lambda l, i: (l, 0, 0))],
        out_shape=[jax.ShapeDtypeStruct((depth, n_main, k), BF16),
                   jax.ShapeDtypeStruct((depth, LANES, k), BF16)],
        scratch_shapes=[pltpu.VMEM((2, rows, k), F32), pltpu.VMEM((LANES, k), F32),
                        pltpu.SemaphoreType.DMA((2,)), pltpu.SemaphoreType.DMA(())],
        compiler_params=_cparams("arbitrary", "arbitrary"),
        name="regroup_w_in",
    )(w_t)


def _scan_kernel(s_ref, bias_ref, alog_ref, o_ref, ot_ref, carry_ref, *, heads, rows):
    @pl.when(pl.program_id(0) == 0)
    def _():
        carry_ref[...] = jnp.zeros_like(carry_ref)

    h = heads
    lane = lax.broadcasted_iota(jnp.int32, (1, LANES), 1)
    ri = lax.broadcasted_iota(jnp.int32, (GDN_CHUNK, GDN_CHUNK), 0)
    ci = lax.broadcasted_iota(jnp.int32, (GDN_CHUNK, GDN_CHUNK), 1)
    tri = jnp.where(ri >= ci, 1.0, 0.0).astype(F32)
    a_exp = jnp.exp(alog_ref[...])
    for sb in range(rows // GDN_CHUNK):
        r0 = sb * GDN_CHUNK
        x = s_ref[r0:r0 + GDN_CHUNK, :] + bias_ref[...]
        soft = jnp.log(1.0 + jnp.exp(-jnp.abs(x)))
        log_sig = jnp.minimum(x, 0.0) - soft
        softplus = jnp.maximum(x, 0.0) + soft
        val = jnp.where(lane < h, log_sig * LOG2E, jnp.where(lane < 2 * h, -a_exp * softplus, 0.0))
        cs = jnp.dot(tri, val, preferred_element_type=F32, precision=lax.Precision.HIGHEST)
        run = cs + carry_ref[...]
        carry_ref[...] = run[GDN_CHUNK - 1:GDN_CHUNK, :]
        g_last = jnp.broadcast_to(cs[GDN_CHUNK - 1:GDN_CHUNK, :], cs.shape)
        g_last = pltpu.roll(g_last, 2 * h, axis=1)
        out = jnp.where(lane < h, run,
                        jnp.where(lane < 2 * h, cs,
                                  jnp.where(lane < 3 * h, _sigmoid(x),
                                            jnp.where(lane < 4 * h, g_last, 0.0))))
        o_ref[r0:r0 + GDN_CHUNK, :] = out
        ot_ref[:, r0:r0 + GDN_CHUNK] = out.T


def _gate_scalars(small, b_f, a_log, dt_bias):
    s = small.shape[0]
    h = b_f.shape[0]
    assert 4 * h <= LANES and s % GDN_CHUNK == 0
    rows = _pick(s, 512, GDN_CHUNK)
    pad = jnp.zeros((LANES - 2 * h,), F32)
    bias = jnp.concatenate([b_f, dt_bias, pad]).reshape(1, LANES)
    alog = jnp.concatenate([jnp.zeros((h,), F32), a_log, pad]).reshape(1, LANES)
    return pl.pallas_call(
        functools.partial(_scan_kernel, heads=h, rows=rows),
        grid=(s // rows,),
        in_specs=[pl.BlockSpec((rows, LANES), lambda i: (i, 0)),
                  pl.BlockSpec((1, LANES), lambda i: (0, 0)),
                  pl.BlockSpec((1, LANES), lambda i: (0, 0))],
        out_specs=[pl.BlockSpec((rows, LANES), lambda i: (i, 0)),
                   pl.BlockSpec((LANES, rows), lambda i: (0, i))],
        out_shape=[jax.ShapeDtypeStruct((s, LANES), F32),
                   jax.ShapeDtypeStruct((LANES, s), F32)],
        scratch_shapes=[pltpu.VMEM((1, LANES), F32)],
        compiler_params=_cparams("arbitrary"),
        name="gate_scalars",
    )(small, bias, alog)


def _fox_prep_kernel(q_ref, k_ref, v_ref, p_ref, qa_ref, ka_ref, vt_ref, *, heads):
    tm = q_ref.shape[0]
    scale = HEAD ** -0.5 * LOG2E
    lane = lax.broadcasted_iota(jnp.int32, (tm, HEAD), 1)
    row = lax.broadcasted_iota(jnp.int32, (HEAD, tm), 0)
    ones_rows = jnp.where(row < 3, 1.0, 0.0).astype(BF16)
    sum_row = jnp.where(lax.broadcasted_iota(jnp.int32, (VT_ROWS - HEAD, tm), 0) == 0, 1.0, 0.0)
    p = p_ref[...]
    for h in range(heads):
        sl = slice(h * HEAD, (h + 1) * HEAD)
        qa_ref[h, 0:HEAD, :] = (q_ref[:, sl].astype(F32) * scale).T.astype(BF16)
        qa_ref[h, HEAD:2 * HEAD, :] = ones_rows
        vt_ref[h, 0] = jnp.concatenate([v_ref[:, sl].astype(F32).T, sum_row], axis=0).astype(BF16)
        f_col = p[:, h:h + 1]
        e = f_col[0:1, :] - f_col
        e_hi = e.astype(BF16).astype(F32)
        e_mid = (e - e_hi).astype(BF16).astype(F32)
        e_lo = e - e_hi - e_mid
        cols = jnp.where(lane == 0, e_hi, jnp.where(lane == 1, e_mid, jnp.where(lane == 2, e_lo, 0.0)))
        ka_ref[h, 0, :, 0:HEAD] = k_ref[:, sl]
        ka_ref[h, 0, :, HEAD:2 * HEAD] = cols.astype(BF16)


def _fox_prep(proj, p, heads, blk):
    s = proj.shape[0]
    w = heads * HEAD
    return pl.pallas_call(
        functools.partial(_fox_prep_kernel, heads=heads),
        grid=(s // blk,),
        in_specs=[pl.BlockSpec((blk, w), lambda i: (i, 0)),
                  pl.BlockSpec((blk, w), lambda i: (i, 1)),
                  pl.BlockSpec((blk, w), lambda i: (i, 2)),
                  pl.BlockSpec((blk, LANES), lambda i: (i, 0))],
        out_specs=[pl.BlockSpec((heads, 2 * HEAD, blk), lambda i: (0, 0, i)),
                   pl.BlockSpec((heads, 1, blk, 2 * HEAD), lambda i: (0, i, 0, 0)),
                   pl.BlockSpec((heads, 1, VT_ROWS, blk), lambda i: (0, i, 0, 0))],
        out_shape=[jax.ShapeDtypeStruct((heads, 2 * HEAD, s), BF16),
                   jax.ShapeDtypeStruct((heads, s // blk, blk, 2 * HEAD), BF16),
                   jax.ShapeDtypeStruct((heads, s // blk, VT_ROWS, blk), BF16)],
        compiler_params=_cparams("parallel"),
        name="fox_prep",
    )(proj, proj, proj, p)


def _fox_kernel(fs_ref, qa_ref, ka_ref, vt_ref, gain_ref, o_ref, m_ref, acc_ref, *, blk, nsub):
    h = pl.program_id(0)
    i = pl.program_id(1)
    m_ref[...] = jnp.full_like(m_ref, NEG)
    acc_ref[...] = jnp.zeros_like(acc_ref)

    def steps(j, chains):
        ka = ka_ref[0, j]
        vt = vt_ref[0, j]
        f_k = fs_ref[h, j]
        scores = [jnp.dot(ka, qa_ref[0, :, a * blk:(a + 1) * blk], preferred_element_type=F32)
                  for a, _ in chains]
        probs, alphas = [], []
        for (a, diagonal), s in zip(chains, scores):
            if diagonal:
                kr = lax.broadcasted_iota(jnp.int32, s.shape, 0)
                qc = lax.broadcasted_iota(jnp.int32, s.shape, 1)
                s = jnp.where(kr <= qc, s, NEG)
            c = fs_ref[h, i * nsub + a] - f_k
            m_old = m_ref[a]
            m_new = jnp.maximum(m_old, jnp.max(s, axis=0, keepdims=True) + c)
            probs.append(jnp.exp2(s - (m_new - c)).astype(BF16))
            alphas.append(jnp.exp2(m_old - m_new))
            m_ref[a] = m_new
        for (a, _), p, alpha in zip(chains, probs, alphas):
            acc_ref[a] = alpha * acc_ref[a] + jnp.dot(vt, p, preferred_element_type=F32)

    def body(jj, carry):
        for t in range(nsub):
            steps(jj * nsub + t, [(a, False) for a in range(nsub)])
        return carry

    lax.fori_loop(0, i, body, 0)
    for t in range(nsub):
        steps(i * nsub + t, [(a, a == t) for a in range(t, nsub)])
    for a in range(nsub):
        acc = acc_ref[a]
        out = acc[0:HEAD] / acc[HEAD:HEAD + 1]
        out = out * lax.rsqrt(jnp.mean(out * out, axis=0, keepdims=True) + EPS)
        o_ref[a * blk:(a + 1) * blk, :] = (out.T * gain_ref[...]).astype(o_ref.dtype)


def _fox_attention(qa, ka, vt, fs, gain, blk):
    heads, _, s = qa.shape
    nk = s // blk
    nsub = NSUB if nk % NSUB == 0 else 1
    tq = nsub * blk
    vt_rows = vt.shape[2]
    grid_spec = pltpu.PrefetchScalarGridSpec(
        num_scalar_prefetch=1,
        grid=(heads, s // tq),
        in_specs=[pl.BlockSpec((1, 2 * HEAD, tq), lambda h, i, fs: (h, 0, i)),
                  pl.BlockSpec((1, nk, blk, 2 * HEAD), lambda h, i, fs: (h, 0, 0, 0)),
                  pl.BlockSpec((1, nk, vt_rows, blk), lambda h, i, fs: (h, 0, 0, 0)),
                  pl.BlockSpec((1, HEAD), lambda h, i, fs: (0, 0))],
        out_specs=pl.BlockSpec((tq, HEAD), lambda h, i, fs: (i, h)),
        scratch_shapes=[pltpu.VMEM((nsub, 1, blk), F32), pltpu.VMEM((nsub, vt_rows, blk), F32)],
    )
    return pl.pallas_call(
        functools.partial(_fox_kernel, blk=blk, nsub=nsub),
        grid_spec=grid_spec,
        out_shape=jax.ShapeDtypeStruct((s, heads * HEAD), BF16),
        compiler_params=_cparams("parallel", "arbitrary"),
        name="fox_attention",
    )(fs, qa, ka, vt, gain.reshape(1, HEAD))


def _gdn_prep_kernel(x_ref, prev_ref, w_ref, o_ref, buf_ref, *, heads):
    tm = x_ref.shape[0]
    halo = prev_ref.shape[0]
    prev = prev_ref[...].astype(F32)
    buf_ref[0:halo, :] = jnp.where(pl.program_id(0) == 0, 0.0, prev)
    buf_ref[halo:halo + tm, :] = x_ref[...].astype(F32)
    y = jnp.zeros(x_ref.shape, F32)
    for s in range(CONV):
        y = y + buf_ref[halo - s:halo - s + tm, :] * w_ref[CONV - 1 - s:CONV - s, :]
    y = y * _sigmoid(y)
    w = heads * HEAD
    for h in range(2 * heads):
        sl = slice(h * HEAD, (h + 1) * HEAD)
        v = y[:, sl]
        n = v * lax.rsqrt(jnp.sum(v * v, axis=-1, keepdims=True) + EPS)
        if h < heads:
            n = n * HEAD ** -0.5
        o_ref[:, sl] = n.astype(o_ref.dtype)
    o_ref[:, 2 * w:3 * w] = y[:, 2 * w:3 * w].astype(o_ref.dtype)


def _gdn_prep(proj, conv_w, heads):
    s = proj.shape[0]
    w3 = 3 * heads * HEAD
    tm = _pick(s, 256, 16)
    halo = 16
    per = tm // halo
    return pl.pallas_call(
        functools.partial(_gdn_prep_kernel, heads=heads),
        grid=(s // tm,),
        in_specs=[pl.BlockSpec((tm, w3), lambda i: (i, 1)),
                  pl.BlockSpec((halo, w3), lambda i: (jnp.maximum(i * per - 1, 0), 1)),
                  pl.BlockSpec((CONV, w3), lambda i: (0, 0))],
        out_specs=pl.BlockSpec((tm, w3), lambda i: (i, 0)),
        out_shape=jax.ShapeDtypeStruct((s, w3), BF16),
        scratch_shapes=[pltpu.VMEM((tm + halo, w3), F32)],
        compiler_params=_cparams("parallel"),
        name="gdn_prep",
    )(proj, proj, conv_w)


def _bdot(a, b):
    return jnp.dot(a.astype(BF16), b.astype(BF16), preferred_element_type=F32)


def _bdot_nt(a, b):
    return lax.dot_general(a.astype(BF16), b.astype(BF16), (((1,), (1,)), ((), ())),
                           preferred_element_type=F32)


def _gdn_kernel(q_ref, k_ref, v_ref, z_ref, pt_ref, gain_ref, o_ref, state_ref, *, heads, group):
    hg = pl.program_id(0)

    @pl.when(pl.program_id(1) == 0)
    def _():
        state_ref[...] = jnp.zeros_like(state_ref)

    tb = q_ref.shape[0]
    c = GDN_CHUNK
    ri = lax.broadcasted_iota(jnp.int32, (tb, tb), 0)
    ci = lax.broadcasted_iota(jnp.int32, (tb, tb), 1)
    rc_xor = jnp.bitwise_xor(ri, ci)
    same_chunk = (rc_xor >> (c.bit_length() - 1)) == 0
    causal = same_chunk & (ri >= ci)
    eye = jnp.where(ri == ci, 1.0, 0.0).astype(F32)

    def col(r):
        return jnp.broadcast_to(r, (HEAD, tb)).T

    hs = range(group)
    sls = [slice(gi * HEAD, (gi + 1) * HEAD) for gi in hs]
    g_row = [pt_ref[pl.ds(heads + hg * group + gi, 1), :] for gi in hs]
    b_row = [pt_ref[pl.ds(2 * heads + hg * group + gi, 1), :] for gi in hs]
    gl_row = [pt_ref[pl.ds(3 * heads + hg * group + gi, 1), :] for gi in hs]
    g_col = [col(r) for r in g_row]
    b_col = [col(r) for r in b_row]
    e_g = [jnp.exp(x) for x in g_col]
    e_tail = [jnp.exp(col(gl_row[gi]) - g_col[gi]) for gi in hs]
    q = [q_ref[:, sl].astype(F32) for sl in sls]
    k = [k_ref[:, sl].astype(F32) for sl in sls]
    v = [v_ref[:, sl].astype(F32) for sl in sls]
    kb = [k[gi] * b_col[gi] for gi in hs]
    kk = [_bdot_nt(kb[gi], k[gi]) for gi in hs]
    qk = [_bdot_nt(q[gi], k[gi]) for gi in hs]
    decay = [jnp.exp(jnp.where(causal, g_col[gi][:, 0:1] - g_row[gi], NEG)) for gi in hs]
    lower = [jnp.where(ri > ci, kk[gi] * decay[gi], 0.0) for gi in hs]
    attn = [qk[gi] * decay[gi] for gi in hs]
    inv = [eye - jnp.where(rc_xor == 1, lower[gi], 0.0) for gi in hs]
    b = 2
    while b < c:
        join = (rc_xor >> (b.bit_length() - 1)) == 1
        if b < 8:
            t = [_bdot(jnp.where(join, lower[gi], 0.0), inv[gi]) for gi in hs]
            inv = [inv[gi] - _bdot(inv[gi], t[gi]) for gi in hs]
        else:
            ng = tb // (2 * b)

            def gather(x):
                return jnp.concatenate([x[g * 2 * b + b:g * 2 * b + 2 * b] for g in range(ng)], axis=0)

            def scatter(xr, base):
                parts = []
                for g in range(ng):
                    parts.append(base[g * 2 * b:g * 2 * b + b])
                    parts.append(xr[g * b:(g + 1) * b])
                return jnp.concatenate(parts, axis=0)

            zeros = jnp.zeros((tb, tb), F32)
            t_r = [_bdot(gather(jnp.where(join, lower[gi], 0.0)), inv[gi]) for gi in hs]
            inv_r = [gather(inv[gi]) for gi in hs]
            upd = [_bdot(inv_r[gi], scatter(t_r[gi], zeros)) for gi in hs]
            inv = [scatter(inv_r[gi] - upd[gi], inv[gi]) for gi in hs]
        b *= 2
    uw = [_bdot(inv[gi], jnp.concatenate([v[gi] * b_col[gi], kb[gi] * e_g[gi]], axis=1)) for gi in hs]
    qg = [q[gi] * e_g[gi] for gi in hs]
    kt_t = [(k[gi] * e_tail[gi]).T for gi in hs]
    state = [state_ref[gi] for gi in hs]
    outs = [[] for _ in hs]
    for ch in range(tb // c):
        r = slice(ch * c, (ch + 1) * c)
        x = [_bdot(jnp.concatenate([uw[gi][r, HEAD:2 * HEAD], qg[gi][r]], axis=0), state[gi]) for gi in hs]
        v_new = [uw[gi][r, 0:HEAD] - x[gi][0:c] for gi in hs]
        for gi in hs:
            outs[gi].append(x[gi][c:2 * c] + _bdot(attn[gi][r, r], v_new[gi]))
        state = [state[gi] * jnp.exp(gl_row[gi][:, ch * c:ch * c + 1]) + _bdot(kt_t[gi][:, r], v_new[gi]) for gi in hs]
    for gi in hs:
        state_ref[gi] = state[gi]
        o = jnp.concatenate(outs[gi], axis=0)
        z = z_ref[:, sls[gi]].astype(F32)
        o = _rms(o) * gain_ref[...] * (z * _sigmoid(z))
        o_ref[:, sls[gi]] = o.astype(o_ref.dtype)


def _gdn(qkv, proj, pt, gain, heads, group=GDN_GROUP):
    s = qkv.shape[0]
    tb = _pick(s, 256, GDN_CHUNK)
    group = min(group, heads)
    gw = group * HEAD
    per = heads // group
    return pl.pallas_call(
        functools.partial(_gdn_kernel, heads=heads, group=group),
        grid=(per, s // tb),
        in_specs=[pl.BlockSpec((tb, gw), lambda g, i: (i, g)),
                  pl.BlockSpec((tb, gw), lambda g, i: (i, per + g)),
                  pl.BlockSpec((tb, gw), lambda g, i: (i, 2 * per + g)),
                  pl.BlockSpec((tb, gw), lambda g, i: (i, 6 * per + g)),
                  pl.BlockSpec((LANES, tb), lambda g, i: (0, i)),
                  pl.BlockSpec((1, HEAD), lambda g, i: (0, 0))],
        out_specs=pl.BlockSpec((tb, gw), lambda g, i: (i, g)),
        out_shape=jax.ShapeDtypeStruct((s, heads * HEAD), BF16),
        scratch_shapes=[pltpu.VMEM((group, HEAD, HEAD), F32)],
        compiler_params=_cparams("parallel", "arbitrary"),
        name="gdn",
    )(qkv, qkv, qkv, proj, pt, gain.reshape(1, HEAD))


def _ffn_kernel(h_ref, wg_ref, wu_ref, wd_ref, o_ref, acc_ref):
    f = pl.program_id(1)

    @pl.when(f == 0)
    def _():
        acc_ref[...] = jnp.zeros_like(acc_ref)

    h = h_ref[...]
    a = jnp.dot(h, wg_ref[...], preferred_element_type=F32)
    b = jnp.dot(h, wu_ref[...], preferred_element_type=F32)
    acc_ref[...] += jnp.dot((a * _sigmoid(a) * b).astype(BF16), wd_ref[...], preferred_element_type=F32)

    @pl.when(f == pl.num_programs(1) - 1)
    def _():
        o_ref[...] = acc_ref[...].astype(o_ref.dtype)


def _ffn_dense(h, wg, wu, wd, layer, out_dtype):
    s, d = h.shape
    f = wg.shape[2]
    tm = _pick(s, 512)
    tf = _pick(f, 512)
    return pl.pallas_call(
        _ffn_kernel,
        grid=(s // tm, f // tf),
        in_specs=[pl.BlockSpec((tm, d), lambda i, j: (i, 0), pipeline_mode=pl.Buffered(1)),
                  pl.BlockSpec((None, d, tf), lambda i, j: (layer, 0, j)),
                  pl.BlockSpec((None, d, tf), lambda i, j: (layer, 0, j)),
                  pl.BlockSpec((None, tf, d), lambda i, j: (layer, j, 0))],
        out_specs=pl.BlockSpec((tm, d), lambda i, j: (i, 0)),
        out_shape=jax.ShapeDtypeStruct((s, d), out_dtype),
        scratch_shapes=[pltpu.VMEM((tm, d), F32)],
        compiler_params=_cparams("parallel", "arbitrary"),
        name="ffn_dense",
    )(h, wg, wu, wd)


def _router_kernel(h_ref, w_ref, o_ref, *, experts):
    lo, hi = _unpack_pairs(h_ref[...])
    half = lo.shape[1]
    logits = (jnp.dot(lo.astype(BF16), w_ref[0:half, :], preferred_element_type=F32)
              + jnp.dot(hi.astype(BF16), w_ref[half:2 * half, :], preferred_element_type=F32))
    lane = lax.broadcasted_iota(jnp.int32, logits.shape, 1).astype(F32)
    logits = jnp.where(lane < experts, logits, -jnp.inf)
    m1 = jnp.max(logits, axis=-1, keepdims=True)
    i1 = jnp.min(jnp.where(logits == m1, lane, LANES), axis=-1, keepdims=True)
    rest = jnp.where(lane == i1, -jnp.inf, logits)
    m2 = jnp.max(rest, axis=-1, keepdims=True)
    i2 = jnp.min(jnp.where(rest == m2, lane, LANES), axis=-1, keepdims=True)
    e2 = jnp.exp(m2 - m1)
    w1 = 1.0 / (1.0 + e2)
    w2 = e2 / (1.0 + e2)
    hit = jnp.where((lane == i1) | (lane == i2), 1.0, 0.0)
    info = jnp.where(lane == experts, i1,
                     jnp.where(lane == experts + 1, i2,
                               jnp.where(lane == experts + 2, w1,
                                         jnp.where(lane == experts + 3, w2, hit))))
    o_ref[...] = info


def _router(h, w_router):
    s, dh = h.shape
    d = 2 * dh
    e = w_router.shape[1]
    assert e + 4 <= LANES and w_router.shape[0] == d
    wr = jnp.zeros((d, LANES), BF16).at[:, :e].set(w_router.astype(BF16))
    tm = _pick(s, 256, 8)
    return pl.pallas_call(
        functools.partial(_router_kernel, experts=e),
        grid=(s // tm,),
        in_specs=[pl.BlockSpec((tm, dh), lambda i: (i, 0)),
                  pl.BlockSpec((d, LANES), lambda i: (0, 0))],
        out_specs=pl.BlockSpec((tm, LANES), lambda i: (i, 0)),
        out_shape=jax.ShapeDtypeStruct((s, LANES), F32),
        compiler_params=_cparams("parallel"),
        name="router",
    )(h, wr)


def _count_kernel(r_ref, o_ref, carry_ref, *, experts):
    @pl.when(pl.program_id(0) == 0)
    def _():
        carry_ref[...] = jnp.zeros_like(carry_ref)

    rows = r_ref.shape[0]
    lane = lax.broadcasted_iota(jnp.int32, (1, LANES), 1)
    ri = lax.broadcasted_iota(jnp.int32, (rows, rows), 0)
    ci = lax.broadcasted_iota(jnp.int32, (rows, rows), 1)
    tri = jnp.where(ri >= ci, 1.0, 0.0).astype(BF16)
    hit = jnp.where(lane < experts, r_ref[...], 0.0)
    run = jnp.dot(tri, hit.astype(BF16), preferred_element_type=F32) + carry_ref[...]
    carry_ref[...] = run[rows - 1:rows, :]
    o_ref[...] = run


def _running_counts(info, experts):
    s = info.shape[0]
    rows = _pick(s, 256, 8)
    return pl.pallas_call(
        functools.partial(_count_kernel, experts=experts),
        grid=(s // rows,),
        in_specs=[pl.BlockSpec((rows, LANES), lambda i: (i, 0))],
        out_specs=pl.BlockSpec((rows, LANES), lambda i: (i, 0)),
        out_shape=jax.ShapeDtypeStruct((s, LANES), F32),
        scratch_shapes=[pltpu.VMEM((1, LANES), F32)],
        compiler_params=_cparams("arbitrary"),
        name="running_counts",
    )(info)


def _row_copy(src, dst, sem, src_row, dst_row):
    return pltpu.make_async_copy(src.at[pl.ds(src_row, 1)], dst.at[pl.ds(dst_row, 1)], sem)


def _dispatch_kernel(pos_ref, h_ref, init_ref, o_ref, sem):
    del init_ref
    tm = h_ref.shape[0]

    def issue(t, carry):
        for slot in range(TOP_K):
            _row_copy(h_ref, o_ref, sem, t, pos_ref[TOP_K * t + slot]).start()
        return carry

    lax.fori_loop(0, tm, issue, 0)

    def drain(t, carry):
        for slot in range(TOP_K):
            _row_copy(h_ref, o_ref, sem, t, pos_ref[TOP_K * t + slot]).wait()
        return carry

    lax.fori_loop(0, tm, drain, 0)


def _dispatch(h, pos_flat, rows_padded):
    s, d = h.shape
    tm = _pick(s, 256, 8)
    init = jnp.zeros((rows_padded, d), h.dtype)
    return pl.pallas_call(
        _dispatch_kernel,
        grid=(s // tm,),
        in_specs=[pl.BlockSpec((TOP_K * tm,), lambda i: (i,), memory_space=pltpu.SMEM),
                  pl.BlockSpec((tm, d), lambda i: (i, 0)),
                  pl.BlockSpec(memory_space=pl.ANY)],
        out_specs=pl.BlockSpec(memory_space=pl.ANY),
        out_shape=jax.ShapeDtypeStruct((rows_padded, d), h.dtype),
        scratch_shapes=[pltpu.SemaphoreType.DMA(())],
        input_output_aliases={2: 0},
        compiler_params=_cparams("arbitrary"),
        name="moe_dispatch",
    )(pos_flat, h, init)


def _moe_kernel(te_ref, used_ref, h_ref, wg_ref, wu_ref, wd_ref, o_ref, hb_ref, acc_ref):
    del te_ref
    i = pl.program_id(0)
    f = pl.program_id(1)
    active = i < used_ref[0]
    half = h_ref.shape[1]

    @pl.when(f == 0)
    def _():
        lo, hi = _unpack_pairs(h_ref[...])
        hb_ref[:, 0:half] = lo.astype(BF16)
        hb_ref[:, half:2 * half] = hi.astype(BF16)
        acc_ref[...] = jnp.zeros_like(acc_ref)

    @pl.when(active)
    def _():
        h = hb_ref[...]
        a = jnp.dot(h, wg_ref[...], preferred_element_type=F32)
        b = jnp.dot(h, wu_ref[...], preferred_element_type=F32)
        acc_ref[...] += jnp.dot((a * _sigmoid(a) * b).astype(BF16), wd_ref[...], preferred_element_type=F32)

    @pl.when(f == pl.num_programs(1) - 1)
    def _():
        o_ref[...] = _pack_pairs(acc_ref[...])


def _moe_grouped(hs, tile_expert, tiles_used, wg, wu, wd, layer, tm):
    rows, dh = hs.shape
    d = 2 * dh
    f = wg.shape[3]
    tf = _pick(f, 256)
    grid_spec = pltpu.PrefetchScalarGridSpec(
        num_scalar_prefetch=2,
        grid=(rows // tm, f // tf),
        in_specs=[pl.BlockSpec((tm, dh), lambda i, j, te, nu: (i, 0)),
                  pl.BlockSpec((None, None, d, tf), lambda i, j, te, nu: (layer, te[i], 0, j)),
                  pl.BlockSpec((None, None, d, tf), lambda i, j, te, nu: (layer, te[i], 0, j)),
                  pl.BlockSpec((None, None, tf, d), lambda i, j, te, nu: (layer, te[i], j, 0))],
        out_specs=pl.BlockSpec((tm, dh), lambda i, j, te, nu: (i, 0)),
        scratch_shapes=[pltpu.VMEM((tm, d), BF16), pltpu.VMEM((tm, d), F32)],
    )
    return pl.pallas_call(
        _moe_kernel,
        grid_spec=grid_spec,
        out_shape=jax.ShapeDtypeStruct((rows, dh), jnp.uint32),
        compiler_params=_cparams("arbitrary", "arbitrary"),
        name="moe_grouped",
    )(tile_expert, tiles_used, hs, wg, wu, wd)


def _combine_kernel(pos_ref, x_ref, info_ref, gpost_ref, mod_ref, gpre_ref, modn_ref, ys_ref,
                    xo_ref, *rest, experts, gate_row, shift_row, scale_row):
    *h_refs, buf_ref, sem = rest
    tm = x_ref.shape[0]

    def issue(t, carry):
        for slot in range(TOP_K):
            _row_copy(ys_ref, buf_ref.at[slot], sem, pos_ref[TOP_K * t + slot], t).start()
        return carry

    lax.fori_loop(0, tm, issue, 0)

    def drain(t, carry):
        for slot in range(TOP_K):
            _row_copy(ys_ref, buf_ref.at[slot], sem, pos_ref[TOP_K * t + slot], t).wait()
        return carry

    lax.fori_loop(0, tm, drain, 0)
    info = info_ref[...]
    w1 = info[:, experts + 2:experts + 3]
    w2 = info[:, experts + 3:experts + 4]
    lo1, hi1 = _unpack_pairs(buf_ref[0])
    lo2, hi2 = _unpack_pairs(buf_ref[1])
    y = jnp.concatenate([w1 * lo1 + w2 * lo2, w1 * hi1 + w2 * hi2], axis=1)
    x = _resid_update(x_ref[...], y, gpost_ref[...], mod_ref[...], gate_row)
    xo_ref[...] = x
    if h_refs:
        h_ref, = h_refs
        _store_h(h_ref, _next_h(x, gpre_ref[...], modn_ref[...], shift_row, scale_row))


def _moe_combine_resid(x, ys, info, pos_flat, experts, gpost, mod, gate_row,
                       gpre=None, mod_next=None, shift_row=0, scale_row=0, h_dtype=None):
    s, d = x.shape
    tm = _pick(s, 256, 8)
    want_h = h_dtype is not None
    if not want_h:
        gpre, mod_next = gpost, mod
    row = pl.BlockSpec((tm, d), lambda i: (i, 0))
    vec = pl.BlockSpec((1, d), lambda i: (0, 0))
    tab = pl.BlockSpec((N_MOD, d), lambda i: (0, 0))
    out_shape = [jax.ShapeDtypeStruct((s, d), F32)]
    out_specs = [row]
    if want_h:
        out_shape.append(jax.ShapeDtypeStruct((s, d), h_dtype))
        out_specs.append(row)
    res = pl.pallas_call(
        functools.partial(_combine_kernel, experts=experts, gate_row=gate_row,
                          shift_row=shift_row, scale_row=scale_row),
        grid=(s // tm,),
        in_specs=[pl.BlockSpec((TOP_K * tm,), lambda i: (i,), memory_space=pltpu.SMEM),
                  row,
                  pl.BlockSpec((tm, LANES), lambda i: (i, 0)),
                  vec, tab, vec, tab,
                  pl.BlockSpec(memory_space=pl.ANY)],
        out_specs=out_specs,
        out_shape=out_shape,
        scratch_shapes=[pltpu.VMEM((TOP_K, tm, d // 2), jnp.uint32), pltpu.SemaphoreType.DMA(())],
        compiler_params=_cparams("arbitrary"),
        name="moe_combine",
    )(pos_flat, x, info, gpost.reshape(1, d), mod, gpre.reshape(1, d), mod_next, ys)
    return (res[0], res[1]) if want_h else (res[0], None)


def _dispatch_table(info, counts, experts, tm):
    s = info.shape[0]
    idx = info[:, experts:experts + TOP_K].astype(jnp.int32)
    rank = jnp.take_along_axis(counts[:, :experts], idx, axis=1).astype(jnp.int32) - 1
    total = counts[s - 1, :experts].astype(jnp.int32)
    padded = ((total + tm - 1) // tm) * tm
    ends = jnp.cumsum(padded)
    starts = ends - padded
    pos = (starts[idx] + rank).reshape(-1)
    n_tiles = (TOP_K * s) // tm + experts
    tile_start = jnp.arange(n_tiles, dtype=jnp.int32) * tm
    tile_expert = jnp.minimum(jnp.sum(tile_start[:, None] >= ends[None, :], axis=1), experts - 1)
    tiles_used = (ends[experts - 1] // tm).reshape(1)
    return pos.astype(jnp.int32), tile_expert.astype(jnp.int32), tiles_used.astype(jnp.int32), n_tiles * tm


def kernel(x, c, w_c, b_c, mod_table, pre_mix_norm, post_mix_norm, pre_ffn_norm, post_ffn_norm, w_in, w_out, conv_w, b_f, a_log, dt_bias, fox_norm, gdn_norm, w_gate_dense, w_up_dense, w_down_dense, w_router, w_gate_moe, w_up_moe, w_down_moe):
    batch, seq, d = x.shape
    assert batch == 1
    depth = mod_table.shape[0]
    half = d // 2
    heads = half // HEAD
    experts = w_router.shape[2]
    blk = _pick(seq, 512)

    o_ff = 3 * half
    o_g = o_ff + heads
    o_ga = o_g + 3 * half
    o_gz = o_ga + 2 * heads
    w_main, w_small = _regroup_w_in(jnp.swapaxes(w_in, 1, 2), ((0, o_ff), (o_g, o_ga), (o_gz, w_in.shape[2])),
                                    ((o_ff, o_g), (o_ga, o_gz)))
    w_out_b = w_out.astype(BF16)
    w_dense = [w.astype(BF16) for w in (w_gate_dense, w_up_dense, w_down_dense)]
    w_moe = [w.astype(BF16) for w in (w_gate_moe, w_up_moe, w_down_moe)]

    mods = _mod_table(c, w_c, b_c, mod_table)
    xs = x.reshape(seq, d)
    moe_tm = _pick(seq, 512)

    h = _prenorm(xs, pre_mix_norm[0], mods[0], 0, 1, BF16)
    for l in range(depth):
        is_moe = l % 2 == 1
        j = l // 2
        proj = _matmul_nt(h, w_main, l, BF16, name="in_proj")
        small = _matmul_nt(h, w_small, l, F32, name="in_proj_small")
        p, pt = _gate_scalars(small, b_f[l], a_log[l], dt_bias[l])
        qa, ka, vt = _fox_prep(proj, p, heads, blk)
        fs = p[::blk, :heads].T
        o_fox = _fox_attention(qa, ka, vt, fs, fox_norm[l], blk)
        qkv = _gdn_prep(proj, conv_w[l], heads)
        o_gdn = _gdn(qkv, proj, pt, gdn_norm[l], heads)
        y = _out_proj(o_fox, o_gdn, w_out_b, l, BF16)
        xs, h = _resid(xs, y, post_mix_norm[l], mods[l], 2, pre_ffn_norm[l], mods[l], 3, 4,
                       jnp.uint32 if is_moe else BF16)
        last = l == depth - 1
        nxt = dict(gpre=None if last else pre_mix_norm[l + 1], mod_next=None if last else mods[l + 1],
                   shift_row=0, scale_row=1, h_dtype=None if last else BF16)
        if not is_moe:
            y = _ffn_dense(h, *w_dense, j, BF16)
            xs, h = _resid(xs, y, post_ffn_norm[l], mods[l], 5, **nxt)
        else:
            info = _router(h, w_router[j])
            counts = _running_counts(info, experts)
            pos, tile_expert, tiles_used, rows_padded = _dispatch_table(info, counts, experts, moe_tm)
            hs = _dispatch(h, pos, rows_padded)
            ys = _moe_grouped(hs, tile_expert, tiles_used, *w_moe, j, moe_tm)
            xs, h = _moe_combine_resid(xs, ys, info, pos, experts, post_ffn_norm[l], mods[l], 5, **nxt)
    return xs.reshape(batch, seq, d)
```
